```python
import math
import jax, jax.numpy as jnp
from jax import lax
import numpy as np

D_MODEL = 1024
BATCH = 16
SEQ = 2048
DEPTH = 4

N_MIXERS = 2
N_FOX = (DEPTH + 1) // 2
N_MLA = DEPTH // 2
FOX_HEADS = 16
FOX_HEAD_DIM = D_MODEL // FOX_HEADS
MLA_HEADS = 16
MLA_NOPE_DIM = D_MODEL // MLA_HEADS
MLA_ROPE_DIM = MLA_NOPE_DIM // 2
MLA_V_DIM = D_MODEL // MLA_HEADS
MLA_Q_RANK = D_MODEL // 4
MLA_KV_RANK = D_MODEL // 8
D_FF = 4 * D_MODEL
Q_BLOCK = 128
ROPE_THETA = 10000.0
NORM_EPS = 1e-6
N_MOD = 6

kernel_name = "hybrid_fox_mla_adaln_trunk"


def rms_norm(x, g):
    x32 = x.astype(jnp.float32)
    y = x32 * lax.rsqrt(jnp.mean(x32 * x32, axis=-1, keepdims=True) + NORM_EPS)
    return y.astype(x.dtype) * g


def causal_block_attention(logits_fn, v):
    S = v.shape[1]
    outs = []
    for qb in range(S // Q_BLOCK):
        q0, q1 = qb * Q_BLOCK, (qb + 1) * Q_BLOCK
        logits = logits_fn(q0, q1)
        allowed = jnp.arange(q1)[None, :] <= jnp.arange(q0, q1)[:, None]
        logits = jnp.where(allowed, logits, -jnp.inf)
        p = jax.nn.softmax(logits, axis=-1).astype(v.dtype)
        outs.append(jnp.einsum('bhqk,bkhd->bqhd', p, v[:, :q1]))
    return jnp.concatenate(outs, axis=1)


def rope_cos_sin(positions, dim):
    inv_freq = ROPE_THETA ** (-jnp.arange(0, dim, 2, dtype=jnp.float32) / dim)
    ang = positions.astype(jnp.float32)[..., None] * inv_freq
    return jnp.cos(ang), jnp.sin(ang)


def apply_rope(x, cos, sin):
    half = x.shape[-1] // 2
    x1, x2 = x[..., :half], x[..., half:]
    cos = cos.astype(x.dtype)
    sin = sin.astype(x.dtype)
    return jnp.concatenate([x1 * cos - x2 * sin, x2 * cos + x1 * sin], axis=-1)


def fox_mixer(h, w_in, b_f, w_out):
    B, S, _ = h.shape
    proj = h @ w_in
    q = proj[..., :D_MODEL].reshape(B, S, FOX_HEADS, FOX_HEAD_DIM)
    k = proj[..., D_MODEL:2 * D_MODEL].reshape(B, S, FOX_HEADS, FOX_HEAD_DIM)
    v = proj[..., 2 * D_MODEL:3 * D_MODEL].reshape(B, S, FOX_HEADS, FOX_HEAD_DIM)
    log_f = jax.nn.log_sigmoid((proj[..., 3 * D_MODEL:] + b_f).astype(jnp.float32))
    F = jnp.cumsum(log_f, axis=1).transpose(0, 2, 1)
    scale = FOX_HEAD_DIM ** -0.5

    def logits_fn(q0, q1):
        s = jnp.einsum('bqhd,bkhd->bhqk', q[:, q0:q1], k[:, :q1],
                       preferred_element_type=jnp.float32)
        return s * scale + (F[:, :, q0:q1, None] - F[:, :, None, :q1])

    o = causal_block_attention(logits_fn, v)
    return o.reshape(B, S, D_MODEL) @ w_out


def mla_mixer(h, cos, sin, w_dq, q_norm_g, w_uq, w_dkv, kv_norm_g, w_ukv, w_out):
    B, S, _ = h.shape
    cq = rms_norm(h @ w_dq, q_norm_g)
    q = (cq @ w_uq).reshape(B, S, MLA_HEADS, MLA_NOPE_DIM + MLA_ROPE_DIM)
    q_nope = q[..., :MLA_NOPE_DIM]
    q_rope = apply_rope(q[..., MLA_NOPE_DIM:], cos[:, :, None, :], sin[:, :, None, :])
    dkv = h @ w_dkv
    ckv = rms_norm(dkv[..., :MLA_KV_RANK], kv_norm_g)
    k_rope = apply_rope(dkv[..., MLA_KV_RANK:], cos, sin)
    kv = (ckv @ w_ukv).reshape(B, S, MLA_HEADS, MLA_NOPE_DIM + MLA_V_DIM)
    k_nope = kv[..., :MLA_NOPE_DIM]
    v = kv[..., MLA_NOPE_DIM:]
    scale = (MLA_NOPE_DIM + MLA_ROPE_DIM) ** -0.5

    def logits_fn(q0, q1):
        s = jnp.einsum('bqhd,bkhd->bhqk', q_nope[:, q0:q1], k_nope[:, :q1],
                       preferred_element_type=jnp.float32)
        s = s + jnp.einsum('bqhd,bkd->bhqk', q_rope[:, q0:q1], k_rope[:, :q1],
                           preferred_element_type=jnp.float32)
        return s * scale

    o = causal_block_attention(logits_fn, v)
    return o.reshape(B, S, MLA_HEADS * MLA_V_DIM) @ w_out


def sq_relu_mlp(h, w1, w2):
    a = jax.nn.relu(h @ w1)
    return (a * a) @ w2


def setup_inputs(seed: int = 0) -> dict:
    key = jax.random.key(seed)
    ks = iter(jax.random.split(key, 32))
    D = D_MODEL

    def nrm(shape, fan_in, mult=1.0):
        return jax.random.normal(next(ks), shape, jnp.float32) * (mult * fan_in ** -0.5)

    def gain(shape):
        return 1.0 + 0.02 * jax.random.normal(next(ks), shape, jnp.float32)

    x = jax.random.normal(next(ks), (BATCH, SEQ, D), jnp.float32)
    c = jax.random.normal(next(ks), (BATCH, D), jnp.float32)
    offs = jax.random.randint(next(ks), (BATCH, 1), 0, 4096, dtype=jnp.int32)
    positions = (jnp.arange(SEQ, dtype=jnp.int32)[None, :] + offs).astype(jnp.int32)
    return {
        "x": x,
        "c": c,
        "positions": positions,
        "ada_w": nrm((DEPTH, D, N_MOD * D), D),
        "ada_b": 0.02 * jax.random.normal(next(ks), (DEPTH, N_MOD * D), jnp.float32),
        "norm_mix_g": gain((DEPTH, D)),
        "norm_mlp_g": gain((DEPTH, D)),
        "fox_w_in": nrm((N_FOX, D, 3 * D + FOX_HEADS), D),
        "fox_b_f": jax.random.uniform(next(ks), (N_FOX, FOX_HEADS), jnp.float32, 2.0, 6.0),
        "fox_w_out": nrm((N_FOX, D, D), D),
        "mla_w_dq": nrm((N_MLA, D, MLA_Q_RANK), D),
        "mla_q_norm_g": gain((N_MLA, MLA_Q_RANK)),
        "mla_w_uq": nrm((N_MLA, MLA_Q_RANK, MLA_HEADS * (MLA_NOPE_DIM + MLA_ROPE_DIM)), MLA_Q_RANK),
        "mla_w_dkv": nrm((N_MLA, D, MLA_KV_RANK + MLA_ROPE_DIM), D),
        "mla_kv_norm_g": gain((N_MLA, MLA_KV_RANK)),
        "mla_w_ukv": nrm((N_MLA, MLA_KV_RANK, MLA_HEADS * (MLA_NOPE_DIM + MLA_V_DIM)), MLA_KV_RANK),
        "mla_w_out": nrm((N_MLA, MLA_HEADS * MLA_V_DIM, D), MLA_HEADS * MLA_V_DIM),
        "mlp_w1": nrm((DEPTH, D, D_FF), D),
        "mlp_w2": nrm((DEPTH, D_FF, D), D_FF),
        "final_norm_g": gain((D,)),
    }


def reference(x, c, positions, ada_w, ada_b, norm_mix_g, norm_mlp_g,
              fox_w_in, fox_b_f, fox_w_out,
              mla_w_dq, mla_q_norm_g, mla_w_uq, mla_w_dkv, mla_kv_norm_g, mla_w_ukv, mla_w_out,
              mlp_w1, mlp_w2, final_norm_g):
    cos, sin = rope_cos_sin(positions, MLA_ROPE_DIM)
    c_act = jax.nn.silu(c)
    for i in range(DEPTH):
        mod = (c_act @ ada_w[i] + ada_b[i])[:, None, :]
        sh_m, sc_m, g_m, sh_f, sc_f, g_f = jnp.split(mod, N_MOD, axis=-1)
        h = rms_norm(x, norm_mix_g[i]) * (1 + sc_m) + sh_m
        j = i // N_MIXERS
        if i % N_MIXERS == 0:
            y = fox_mixer(h, fox_w_in[j], fox_b_f[j], fox_w_out[j])
        else:
            y = mla_mixer(h, cos, sin, mla_w_dq[j], mla_q_norm_g[j], mla_w_uq[j],
                          mla_w_dkv[j], mla_kv_norm_g[j], mla_w_ukv[j], mla_w_out[j])
        x = x + g_m * y
        h = rms_norm(x, norm_mlp_g[i]) * (1 + sc_f) + sh_f
        x = x + g_f * sq_relu_mlp(h, mlp_w1[i], mlp_w2[i])
    return rms_norm(x, final_norm_g)
```

```python
import functools

import jax
import jax.numpy as jnp
from jax import lax
from jax.experimental import pallas as pl
from jax.experimental.pallas import tpu as pltpu

D_MODEL = 1024
DEPTH = 4
HEADS = 16
HEAD_DIM = 64
ROPE_DIM = 32
ROPE_HALF = ROPE_DIM // 2
Q_RANK = 256
KV_RANK = 128
D_FF = 4 * D_MODEL
ROPE_THETA = 10000.0
NORM_EPS = 1e-6
N_MOD = 6

LANES = 128
HP = HEADS * LANES
N_SPLIT = 3

TS = 512
TQ = 256
TK = 256
VMEM_LIMIT = 56 * 1024 * 1024

BF16 = jnp.bfloat16
F32 = jnp.float32
NT_DIMS = (((1,), (1,)), ((), ()))


def _dot(a, b):
    return jnp.dot(a, b, preferred_element_type=F32)


def _dot_nt(a, b):
    return lax.dot_general(a, b, NT_DIMS, preferred_element_type=F32)


def _dot_exact(a, b):
    return jnp.dot(a, b, preferred_element_type=F32, precision=lax.Precision.HIGHEST)


def _rms(x, g):
    ms = jnp.mean(x * x, axis=-1, keepdims=True)
    return x * lax.rsqrt(ms + NORM_EPS) * g


def _const_spec(shape):
    zeros = (0,) * len(shape)
    return pl.BlockSpec(shape, lambda *_: zeros, pipeline_mode=pl.Buffered(1))


def _params(n_axes):
    return pltpu.CompilerParams(
        dimension_semantics=("arbitrary",) * n_axes,
        vmem_limit_bytes=VMEM_LIMIT)


MOD_TN = 1024


def _mod_kernel(c_ref, w_ref, b_ref, o_ref):
    ca = jax.nn.silu(c_ref[...])
    o_ref[0] = _dot_exact(ca, w_ref[0]) + b_ref[0]


def _mod_call(c, ada_w, ada_b):
    depth, d, n = ada_w.shape
    bsz = c.shape[0]
    return pl.pallas_call(
        _mod_kernel,
        out_shape=jax.ShapeDtypeStruct((depth, bsz, n), F32),
        grid=(depth, n // MOD_TN),
        in_specs=[
            pl.BlockSpec((bsz, d), lambda l, j: (0, 0)),
            pl.BlockSpec((1, d, MOD_TN), lambda l, j: (l, 0, j)),
            pl.BlockSpec((1, 1, MOD_TN), lambda l, j: (l, 0, j)),
        ],
        out_specs=pl.BlockSpec((1, bsz, MOD_TN), lambda l, j: (l, 0, j)),
        compiler_params=_params(2),
        name="adaln_mod",
    )(c, ada_w, ada_b.reshape(depth, 1, n))


def _rope_kernel(pos_col_ref, pos_row_ref, c_ref, s_ref, ct_ref, st_ref):
    seq = pos_col_ref.shape[1]
    lane = lax.broadcasted_iota(jnp.int32, (1, LANES), 1)
    is_rope = (lane >= HEAD_DIM) & (lane < HEAD_DIM + ROPE_DIM)
    fidx = ((lane - HEAD_DIM) % ROPE_HALF).astype(F32)
    inv_freq = ROPE_THETA ** (-(2.0 * fidx) / ROPE_DIM)
    ang = pos_col_ref[0].astype(F32) * inv_freq
    rope_mask = jnp.broadcast_to(is_rope, (seq, LANES))
    pass_mask = jnp.broadcast_to(lane < HEAD_DIM, (seq, LANES))
    c_ref[0] = jnp.where(rope_mask, jnp.cos(ang), jnp.where(pass_mask, 1.0, 0.0))
    s_ref[0] = jnp.where(rope_mask, jnp.sin(ang), 0.0)

    row = lax.broadcasted_iota(jnp.int32, (ROPE_HALF, 1), 0).astype(F32)
    inv_freq_t = ROPE_THETA ** (-(2.0 * row) / ROPE_DIM)
    ang_t = inv_freq_t * pos_row_ref[0].astype(F32)
    ct_ref[0] = jnp.cos(ang_t)
    st_ref[0] = jnp.sin(ang_t)


def _rope_call(positions):
    bsz, seq = positions.shape
    return pl.pallas_call(
        _rope_kernel,
        out_shape=(
            jax.ShapeDtypeStruct((bsz, seq, LANES), F32),
            jax.ShapeDtypeStruct((bsz, seq, LANES), F32),
            jax.ShapeDtypeStruct((bsz, ROPE_HALF, seq), F32),
            jax.ShapeDtypeStruct((bsz, ROPE_HALF, seq), F32),
        ),
        grid=(bsz,),
        in_specs=[
            pl.BlockSpec((1, seq, 1), lambda b: (b, 0, 0)),
            pl.BlockSpec((1, 1, seq), lambda b: (b, 0, 0)),
        ],
        out_specs=(
            pl.BlockSpec((1, seq, LANES), lambda b: (b, 0, 0)),
            pl.BlockSpec((1, seq, LANES), lambda b: (b, 0, 0)),
            pl.BlockSpec((1, ROPE_HALF, seq), lambda b: (b, 0, 0)),
            pl.BlockSpec((1, ROPE_HALF, seq), lambda b: (b, 0, 0)),
        ),
        compiler_params=_params(1),
        name="rope_tables",
    )(positions.reshape(bsz, seq, 1), positions.reshape(bsz, 1, seq))


def _modulated_norm(x_ref, mod_ref, g_ref, shift_row):
    x = x_ref[0]
    sh = mod_ref[0, 0, shift_row:shift_row + 1, :]
    sc = mod_ref[0, 0, shift_row + 1:shift_row + 2, :]
    return _rms(x, g_ref[...]) * (1.0 + sc) + sh


def _split3(x):
    hi = x.astype(BF16).astype(F32)
    r1 = x - hi
    mid = r1.astype(BF16).astype(F32)
    lo = r1 - mid
    return hi, mid, lo


def _log_sigmoid(x):
    return jnp.minimum(x, 0.0) - jnp.log1p(jnp.exp(-jnp.abs(x)))


def _fox_pre_kernel(x_ref, mod_ref, g_ref, wq_ref, wkt_ref, wv_ref, wf_ref, wft_ref,
                    bf_ref, bft_ref, qp_ref, kt_ref, v_ref, carry_ref, carry_t_ref):
    ts = x_ref.shape[1]

    @pl.when(pl.program_id(1) == 0)
    def _():
        carry_ref[...] = jnp.zeros_like(carry_ref)
        carry_t_ref[...] = jnp.zeros_like(carry_t_ref)

    hb = _modulated_norm(x_ref, mod_ref, g_ref, 0).astype(BF16)
    q = _dot(hb, wq_ref[...]) * (HEAD_DIM ** -0.5)
    k_t = _dot_nt(wkt_ref[...], hb)
    v_ref[0] = _dot(hb, wv_ref[...]).astype(BF16)

    lf = _log_sigmoid(_dot(hb, wf_ref[...]) + bf_ref[...])
    lf_t = _log_sigmoid(_dot_nt(wft_ref[...], hb) + bft_ref[...])
    r = lax.broadcasted_iota(jnp.int32, (ts, ts), 0)
    c = lax.broadcasted_iota(jnp.int32, (ts, ts), 1)
    lower = jnp.where(c <= r, 1.0, 0.0).astype(F32)
    upper = jnp.where(r <= c, 1.0, 0.0).astype(F32)
    f_col = _dot_exact(lower, lf) + carry_ref[...]
    f_row = _dot_exact(lf_t, upper) + carry_t_ref[...]
    carry_ref[...] = f_col[ts - 1:ts, :]
    carry_t_ref[...] = f_row[:, ts - 1:ts]

    lane = lax.broadcasted_iota(jnp.int32, (ts, LANES), 1)
    sub = lax.broadcasted_iota(jnp.int32, (16, ts), 0)
    for hd in range(HEADS):
        blk = q[:, LANES * (hd // 2):LANES * (hd // 2 + 1)]
        if hd % 2 == 1:
            blk = pltpu.roll(blk, HEAD_DIM, axis=1)
        hi, mid, lo = _split3(f_col[:, hd:hd + 1])
        ext = jnp.where(lane == HEAD_DIM, hi,
              jnp.where(lane == HEAD_DIM + 1, mid,
              jnp.where(lane == HEAD_DIM + 2, lo,
              jnp.where(lane < HEAD_DIM + 2 * N_SPLIT, 1.0, 0.0))))
        qp_ref[0, :, LANES * hd:LANES * (hd + 1)] = (
            jnp.where(lane < HEAD_DIM, blk, ext).astype(BF16))

        row0 = LANES * hd
        kt_ref[0, row0:row0 + HEAD_DIM, :] = (
            k_t[HEAD_DIM * hd:HEAD_DIM * (hd + 1), :].astype(BF16))
        khi, kmid, klo = _split3(f_row[hd:hd + 1, :])
        ext_t = jnp.where(sub == N_SPLIT, -khi,
                jnp.where(sub == N_SPLIT + 1, -kmid,
                jnp.where(sub == N_SPLIT + 2, -klo,
                jnp.where(sub < N_SPLIT, 1.0, 0.0))))
        kt_ref[0, row0 + HEAD_DIM:row0 + HEAD_DIM + 16, :] = ext_t.astype(BF16)
        kt_ref[0, row0 + HEAD_DIM + 16:row0 + LANES, :] = jnp.zeros(
            (LANES - HEAD_DIM - 16, ts), BF16)


def _pre_out(bsz, seq):
    out_shape = (
        jax.ShapeDtypeStruct((bsz, seq, HP), BF16),
        jax.ShapeDtypeStruct((bsz, HP, seq), BF16),
        jax.ShapeDtypeStruct((bsz, seq, D_MODEL), BF16),
    )
    out_specs = (
        pl.BlockSpec((1, TS, HP), lambda b, s: (b, s, 0)),
        pl.BlockSpec((1, HP, TS), lambda b, s: (b, 0, s)),
        pl.BlockSpec((1, TS, D_MODEL), lambda b, s: (b, s, 0)),
    )
    return out_shape, out_specs


def _fox_pre_call(x, mod, layer, g, wq, wkt, wv, wf, wft, bf, bft):
    bsz, seq, d = x.shape
    out_shape, out_specs = _pre_out(bsz, seq)
    return pl.pallas_call(
        _fox_pre_kernel,
        out_shape=out_shape,
        grid=(bsz, seq // TS),
        in_specs=[
            pl.BlockSpec((1, TS, d), lambda b, s: (b, s, 0)),
            pl.BlockSpec((1, 1, N_MOD, d), lambda b, s: (layer, b, 0, 0)),
            _const_spec(g.shape),
            _const_spec(wq.shape), _const_spec(wkt.shape), _const_spec(wv.shape),
            _const_spec(wf.shape), _const_spec(wft.shape),
            _const_spec(bf.shape), _const_spec(bft.shape),
        ],
        out_specs=out_specs,
        scratch_shapes=[pltpu.VMEM((1, HEADS), F32), pltpu.VMEM((HEADS, 1), F32)],
        compiler_params=_params(2),
        name="fox_pre",
    )(x, mod, g, wq, wkt, wv, wf, wft, bf, bft)


def _mla_pre_kernel(x_ref, mod_ref, g_ref, wdq_ref, gq_ref, wuq_ref, wdkv_ref, wdkvr_t_ref,
                    gkv_ref, wuk_t_ref, wuv_ref, cos_ref, sin_ref, cos_t_ref, sin_t_ref,
                    qp_ref, kt_ref, v_ref):
    ts = x_ref.shape[1]
    hb = _modulated_norm(x_ref, mod_ref, g_ref, 0).astype(BF16)

    cq = _rms(_dot(hb, wdq_ref[...]), gq_ref[...]).astype(BF16)
    q = _dot(cq, wuq_ref[...])
    cos = cos_ref[0]
    sin = sin_ref[0]
    lane = lax.broadcasted_iota(jnp.int32, (ts, LANES), 1)
    first_half = lane < HEAD_DIM + ROPE_HALF
    scale = (HEAD_DIM + ROPE_DIM) ** -0.5
    for hd in range(HEADS):
        blk = q[:, LANES * hd:LANES * (hd + 1)]
        rot = jnp.where(first_half,
                        -pltpu.roll(blk, LANES - ROPE_HALF, axis=1),
                        pltpu.roll(blk, ROPE_HALF, axis=1))
        qp_ref[0, :, LANES * hd:LANES * (hd + 1)] = (
            ((blk * cos + rot * sin) * scale).astype(BF16))

    ckv = _rms(_dot(hb, wdkv_ref[...]), gkv_ref[...]).astype(BF16)
    v_ref[0] = _dot(ckv, wuv_ref[...]).astype(BF16)
    kn_t = _dot_nt(wuk_t_ref[...], ckv)
    kr_t = _dot_nt(wdkvr_t_ref[...], hb)
    x1 = kr_t[:ROPE_HALF]
    x2 = kr_t[ROPE_HALF:]
    cos_t = cos_t_ref[0]
    sin_t = sin_t_ref[0]
    kr = jnp.concatenate([x1 * cos_t - x2 * sin_t, x2 * cos_t + x1 * sin_t],
                         axis=0).astype(BF16)
    for hd in range(HEADS):
        row0 = LANES * hd
        kt_ref[0, row0:row0 + HEAD_DIM, :] = (
            kn_t[HEAD_DIM * hd:HEAD_DIM * (hd + 1), :].astype(BF16))
        kt_ref[0, row0 + HEAD_DIM:row0 + HEAD_DIM + ROPE_DIM, :] = kr
        kt_ref[0, row0 + HEAD_DIM + ROPE_DIM:row0 + LANES, :] = jnp.zeros(
            (LANES - HEAD_DIM - ROPE_DIM, ts), BF16)


def _mla_pre_call(x, mod, layer, g, wdq, gq, wuq, wdkv, wdkvr_t, gkv, wuk_t, wuv, tables):
    bsz, seq, d = x.shape
    cos, sin, cos_t, sin_t = tables
    out_shape, out_specs = _pre_out(bsz, seq)
    return pl.pallas_call(
        _mla_pre_kernel,
        out_shape=out_shape,
        grid=(bsz, seq // TS),
        in_specs=[
            pl.BlockSpec((1, TS, d), lambda b, s: (b, s, 0)),
            pl.BlockSpec((1, 1, N_MOD, d), lambda b, s: (layer, b, 0, 0)),
            _const_spec(g.shape),
            _const_spec(wdq.shape), _const_spec(gq.shape), _const_spec(wuq.shape),
            _const_spec(wdkv.shape), _const_spec(wdkvr_t.shape), _const_spec(gkv.shape),
            _const_spec(wuk_t.shape), _const_spec(wuv.shape),
            pl.BlockSpec((1, TS, LANES), lambda b, s: (b, s, 0)),
            pl.BlockSpec((1, TS, LANES), lambda b, s: (b, s, 0)),
            pl.BlockSpec((1, ROPE_HALF, TS), lambda b, s: (b, 0, s)),
            pl.BlockSpec((1, ROPE_HALF, TS), lambda b, s: (b, 0, s)),
        ],
        out_specs=out_specs,
        compiler_params=_params(2),
        name="mla_pre",
    )(x, mod, g, wdq, gq, wuq, wdkv, wdkvr_t, gkv, wuk_t, wuv, cos, sin, cos_t, sin_t)


def _attn_kernel(q_ref, kt_ref, v_ref, o_ref, s_ref, acc_ref, l_ref):
    seq = q_ref.shape[1]
    r = lax.broadcasted_iota(jnp.int32, (TQ, TK), 0)
    c = lax.broadcasted_iota(jnp.int32, (TQ, TK), 1)
    causal = c <= r
    lane = lax.broadcasted_iota(jnp.int32, (TQ, LANES), 1)

    def head_out(qi, q0, h):
        q = q_ref[0, pl.ds(q0, TQ), LANES * h:LANES * (h + 1)]

        def logits(kc, mvec, masked):
            k0 = pl.multiple_of(kc * TK, TK)
            s = _dot(q, kt_ref[0, LANES * h:LANES * (h + 1), pl.ds(k0, TK)])
            if masked:
                s = jnp.where(causal, s, -jnp.inf)
            s_ref[:, pl.ds(k0, TK)] = s
            for j in range(TK // LANES):
                mvec = jnp.maximum(mvec, s[:, LANES * j:LANES * (j + 1)])
            return mvec

        mvec = lax.fori_loop(0, qi, lambda kc, mv: logits(kc, mv, False),
                             jnp.full((TQ, LANES), -jnp.inf, F32))
        mvec = logits(qi, mvec, True)
        m = jnp.max(mvec, axis=1, keepdims=True)

        acc_ref[...] = jnp.zeros_like(acc_ref)
        l_ref[...] = jnp.zeros_like(l_ref)

        def weighted(kc, carry):
            k0 = pl.multiple_of(kc * TK, TK)
            p = jnp.exp(s_ref[:, pl.ds(k0, TK)] - m)
            lsum = p[:, 0:LANES]
            for j in range(1, TK // LANES):
                lsum = lsum + p[:, LANES * j:LANES * (j + 1)]
            l_ref[...] += lsum
            acc_ref[...] += _dot(p.astype(BF16), v_ref[0, pl.ds(k0, TK), :])
            return carry

        lax.fori_loop(0, qi + 1, weighted, 0)
        l = jnp.sum(l_ref[...], axis=1, keepdims=True)
        return acc_ref[...] / l

    def q_block(qi, carry):
        q0 = pl.multiple_of(qi * TQ, TQ)
        o0 = head_out(qi, q0, 0)
        o1 = head_out(qi, q0, 1)
        o_ref[0, pl.ds(q0, TQ), :] = jnp.where(lane < HEAD_DIM, o0, o1).astype(BF16)
        return carry

    lax.fori_loop(0, seq // TQ, q_block, 0)


def _attn_call(qp, kt, v):
    bsz, seq, _ = qp.shape
    return pl.pallas_call(
        _attn_kernel,
        out_shape=jax.ShapeDtypeStruct((bsz, seq, D_MODEL), BF16),
        grid=(bsz, HEADS // 2),
        in_specs=[
            pl.BlockSpec((1, seq, 2 * LANES), lambda b, p: (b, 0, p)),
            pl.BlockSpec((1, 2 * LANES, seq), lambda b, p: (b, p, 0)),
            pl.BlockSpec((1, seq, LANES), lambda b, p: (b, 0, p)),
        ],
        out_specs=pl.BlockSpec((1, seq, LANES), lambda b, p: (b, 0, p)),
        scratch_shapes=[
            pltpu.VMEM((TQ, seq), F32),
            pltpu.VMEM((TQ, LANES), F32),
            pltpu.VMEM((TQ, LANES), F32),
        ],
        compiler_params=_params(2),
        name="causal_attn",
    )(qp, kt, v)


FF_CHUNK = 1024


def _post_kernel(o_ref, x_ref, mod_ref, g_ref, wo_ref, w1_ref, w2_ref, gf_ref, out_ref,
                 *, final):
    g_m = mod_ref[0, 0, 2:3, :]
    sh_f = mod_ref[0, 0, 3:4, :]
    sc_f = mod_ref[0, 0, 4:5, :]
    g_f = mod_ref[0, 0, 5:6, :]
    x1 = x_ref[0] + g_m * _dot(o_ref[0], wo_ref[...])
    hb = (_rms(x1, g_ref[...]) * (1.0 + sc_f) + sh_f).astype(BF16)
    acc = jnp.zeros_like(x1)
    for j in range(D_FF // FF_CHUNK):
        a = jnp.maximum(_dot(hb, w1_ref[:, FF_CHUNK * j:FF_CHUNK * (j + 1)]), 0.0)
        acc = acc + _dot((a * a).astype(BF16), w2_ref[FF_CHUNK * j:FF_CHUNK * (j + 1), :])
    x2 = x1 + g_f * acc
    if final:
        x2 = _rms(x2, gf_ref[...])
    out_ref[0] = x2


def _post_call(o, x, mod, layer, g, wo, w1, w2, gf, final):
    bsz, seq, d = x.shape
    return pl.pallas_call(
        functools.partial(_post_kernel, final=final),
        out_shape=jax.ShapeDtypeStruct((bsz, seq, d), F32),
        grid=(bsz, seq // TS),
        in_specs=[
            pl.BlockSpec((1, TS, d), lambda b, s: (b, s, 0)),
            pl.BlockSpec((1, TS, d), lambda b, s: (b, s, 0)),
            pl.BlockSpec((1, 1, N_MOD, d), lambda b, s: (layer, b, 0, 0)),
            _const_spec(g.shape),
            _const_spec(wo.shape), _const_spec(w1.shape), _const_spec(w2.shape),
            _const_spec(gf.shape),
        ],
        out_specs=pl.BlockSpec((1, TS, d), lambda b, s: (b, s, 0)),
        compiler_params=_params(2),
        name="post_mlp",
    )(o, x, mod, g, wo, w1, w2, gf)


def kernel(x, c, positions, ada_w, ada_b, norm_mix_g, norm_mlp_g, fox_w_in, fox_b_f, fox_w_out, mla_w_dq, mla_q_norm_g, mla_w_uq, mla_w_dkv, mla_kv_norm_g, mla_w_ukv, mla_w_out, mlp_w1, mlp_w2, final_norm_g):
    bsz, seq, d = x.shape
    assert d == D_MODEL and seq % TS == 0 and seq % TQ == 0 and TQ == TK
    assert ada_w.shape == (DEPTH, d, N_MOD * d)
    assert fox_w_in.shape[1:] == (d, 3 * d + HEADS)
    assert mla_w_uq.shape[1:] == (Q_RANK, HEADS * (HEAD_DIM + ROPE_DIM))
    assert mla_w_dkv.shape[1:] == (d, KV_RANK + ROPE_DIM)
    assert mla_w_ukv.shape[1:] == (KV_RANK, HEADS * 2 * HEAD_DIM)

    mod = _mod_call(c, ada_w, ada_b).reshape(DEPTH, bsz, N_MOD, d)
    tables = _rope_call(positions)
    gf = final_norm_g.reshape(1, d)

    for i in range(DEPTH):
        j = i // 2
        g_mix = norm_mix_g[i].reshape(1, d)
        if i % 2 == 0:
            w_in = fox_w_in[j]
            wq = w_in[:, :d].astype(BF16)
            wkt = w_in[:, d:2 * d].T.astype(BF16)
            wv = w_in[:, 2 * d:3 * d].astype(BF16)
            wf = w_in[:, 3 * d:].astype(BF16)
            wft = w_in[:, 3 * d:].T.astype(BF16)
            bf = fox_b_f[j].reshape(1, HEADS)
            bft = fox_b_f[j].reshape(HEADS, 1)
            qp, kt, v = _fox_pre_call(x, mod, i, g_mix, wq, wkt, wv, wf, wft, bf, bft)
            wo = fox_w_out[j].astype(BF16)
        else:
            wdq = mla_w_dq[j].astype(BF16)
            gq = mla_q_norm_g[j].reshape(1, Q_RANK)
            wuq = jnp.pad(
                mla_w_uq[j].reshape(Q_RANK, HEADS, HEAD_DIM + ROPE_DIM),
                ((0, 0), (0, 0), (0, LANES - HEAD_DIM - ROPE_DIM)),
            ).reshape(Q_RANK, HP).astype(BF16)
            wdkv = mla_w_dkv[j][:, :KV_RANK].astype(BF16)
            wdkvr_t = mla_w_dkv[j][:, KV_RANK:].T.astype(BF16)
            gkv = mla_kv_norm_g[j].reshape(1, KV_RANK)
            w_ukv = mla_w_ukv[j].reshape(KV_RANK, HEADS, 2 * HEAD_DIM)
            wuk_t = w_ukv[:, :, :HEAD_DIM].reshape(KV_RANK, HEADS * HEAD_DIM).T.astype(BF16)
            wuv = w_ukv[:, :, HEAD_DIM:].reshape(KV_RANK, HEADS * HEAD_DIM).astype(BF16)
            qp, kt, v = _mla_pre_call(x, mod, i, g_mix, wdq, gq, wuq, wdkv, wdkvr_t, gkv,
                                      wuk_t, wuv, tables)
            wo = mla_w_out[j].astype(BF16)
        o = _attn_call(qp, kt, v)
        x = _post_call(o, x, mod, i, norm_mlp_g[i].reshape(1, d), wo,
                       mlp_w1[i].astype(BF16), mlp_w2[i].astype(BF16), gf,
                       final=(i == DEPTH - 1))
    return x
```

```python
import functools

import jax
import jax.numpy as jnp
from jax import lax
from jax.experimental import pallas as pl
from jax.experimental.pallas import tpu as pltpu

D_MODEL = 1024
DEPTH = 4
HEADS = 16
HEAD_DIM = 64
ROPE_DIM = 32
ROPE_HALF = ROPE_DIM // 2
Q_RANK = 256
KV_RANK = 128
D_FF = 4 * D_MODEL
ROPE_THETA = 10000.0
NORM_EPS = 1e-6
N_MOD = 6

LANES = 128
HP = HEADS * LANES
N_SPLIT = 3

TS = 512
TQ = 256
VMEM_LIMIT = 56 * 1024 * 1024

BF16 = jnp.bfloat16
F32 = jnp.float32
NT_DIMS = (((1,), (1,)), ((), ()))
TN_DIMS = (((0,), (0,)), ((), ()))


def _dot(a, b):
    return jnp.dot(a, b, preferred_element_type=F32)


def _dot_nt(a, b):
    return lax.dot_general(a, b, NT_DIMS, preferred_element_type=F32)


def _dot_tn(a, b):
    return lax.dot_general(a, b, TN_DIMS, preferred_element_type=F32)


def _dot_exact(a, b):
    return jnp.dot(a, b, preferred_element_type=F32, precision=lax.Precision.HIGHEST)


def _rms(x, g):
    ms = jnp.mean(x * x, axis=-1, keepdims=True)
    return x * lax.rsqrt(ms + NORM_EPS) * g


def _const_spec(shape):
    zeros = (0,) * len(shape)
    return pl.BlockSpec(shape, lambda *_: zeros, pipeline_mode=pl.Buffered(1))


def _params(n_axes):
    return pltpu.CompilerParams(
        dimension_semantics=("arbitrary",) * n_axes,
        vmem_limit_bytes=VMEM_LIMIT)


MOD_TN = 1024


def _mod_kernel(c_ref, w_ref, b_ref, o_ref):
    ca = jax.nn.silu(c_ref[...])
    o_ref[0] = _dot_exact(ca, w_ref[0]) + b_ref[0]


def _mod_call(c, ada_w, ada_b):
    depth, d, n = ada_w.shape
    bsz = c.shape[0]
    return pl.pallas_call(
        _mod_kernel,
        out_shape=jax.ShapeDtypeStruct((depth, bsz, n), F32),
        grid=(depth, n // MOD_TN),
        in_specs=[
            pl.BlockSpec((bsz, d), lambda l, j: (0, 0)),
            pl.BlockSpec((1, d, MOD_TN), lambda l, j: (l, 0, j)),
            pl.BlockSpec((1, 1, MOD_TN), lambda l, j: (l, 0, j)),
        ],
        out_specs=pl.BlockSpec((1, bsz, MOD_TN), lambda l, j: (l, 0, j)),
        compiler_params=_params(2),
        name="adaln_mod",
    )(c, ada_w, ada_b.reshape(depth, 1, n))


def _rope_kernel(pos_ref, c_ref, s_ref):
    seq = pos_ref.shape[1]
    lane = lax.broadcasted_iota(jnp.int32, (1, LANES), 1)
    is_rope = (lane >= HEAD_DIM) & (lane < HEAD_DIM + ROPE_DIM)
    fidx = ((lane - HEAD_DIM) % ROPE_HALF).astype(F32)
    inv_freq = ROPE_THETA ** (-(2.0 * fidx) / ROPE_DIM)
    ang = pos_ref[0].astype(F32) * inv_freq
    rope_mask = jnp.broadcast_to(is_rope, (seq, LANES))
    pass_mask = jnp.broadcast_to(lane < HEAD_DIM, (seq, LANES))
    c_ref[0] = jnp.where(rope_mask, jnp.cos(ang), jnp.where(pass_mask, 1.0, 0.0))
    s_ref[0] = jnp.where(rope_mask, jnp.sin(ang), 0.0)


def _rope_call(positions):
    bsz, seq = positions.shape
    return pl.pallas_call(
        _rope_kernel,
        out_shape=(
            jax.ShapeDtypeStruct((bsz, seq, LANES), F32),
            jax.ShapeDtypeStruct((bsz, seq, LANES), F32),
        ),
        grid=(bsz,),
        in_specs=[pl.BlockSpec((1, seq, 1), lambda b: (b, 0, 0))],
        out_specs=(
            pl.BlockSpec((1, seq, LANES), lambda b: (b, 0, 0)),
            pl.BlockSpec((1, seq, LANES), lambda b: (b, 0, 0)),
        ),
        compiler_params=_params(1),
        name="rope_tables",
    )(positions.reshape(bsz, seq, 1))


def _modulated_norm(x_ref, mod_ref, g_ref, shift_row):
    x = x_ref[0]
    sh = mod_ref[0, 0, shift_row:shift_row + 1, :]
    sc = mod_ref[0, 0, shift_row + 1:shift_row + 2, :]
    return _rms(x, g_ref[...]) * (1.0 + sc) + sh


def _split3(x):
    hi = x.astype(BF16).astype(F32)
    r1 = x - hi
    mid = r1.astype(BF16).astype(F32)
    lo = r1 - mid
    return hi, mid, lo


def _log_sigmoid(x):
    return jnp.minimum(x, 0.0) - jnp.log1p(jnp.exp(-jnp.abs(x)))


def _rope_rotate(blk, cos, sin, first_half):
    rot = jnp.where(first_half,
                    -pltpu.roll(blk, LANES - ROPE_HALF, axis=1),
                    pltpu.roll(blk, ROPE_HALF, axis=1))
    return blk * cos + rot * sin


def _fox_pre_kernel(x_ref, mod_ref, g_ref, wq_ref, wk_ref, wvt_ref, wf_ref, bf_ref,
                    qp_ref, kp_ref, vt_ref, carry_ref):
    ts = x_ref.shape[1]

    @pl.when(pl.program_id(1) == 0)
    def _():
        carry_ref[...] = jnp.zeros_like(carry_ref)

    hb = _modulated_norm(x_ref, mod_ref, g_ref, 0).astype(BF16)
    q = _dot(hb, wq_ref[...]) * (HEAD_DIM ** -0.5)
    k = _dot(hb, wk_ref[...])
    vt_ref[0] = _dot_nt(wvt_ref[...], hb).astype(BF16)

    lf = _log_sigmoid(_dot(hb, wf_ref[...]) + bf_ref[...])
    r = lax.broadcasted_iota(jnp.int32, (ts, ts), 0)
    c = lax.broadcasted_iota(jnp.int32, (ts, ts), 1)
    lower = jnp.where(c <= r, 1.0, 0.0).astype(F32)
    f_cum = _dot_exact(lower, lf) + carry_ref[...]
    carry_ref[...] = f_cum[ts - 1:ts, :]

    lane = lax.broadcasted_iota(jnp.int32, (ts, LANES), 1)
    in_head = lane < HEAD_DIM
    for hd in range(HEADS):
        cols = slice(LANES * (hd // 2), LANES * (hd // 2 + 1))
        q_blk = q[:, cols]
        k_blk = k[:, cols]
        if hd % 2 == 1:
            q_blk = pltpu.roll(q_blk, HEAD_DIM, axis=1)
            k_blk = pltpu.roll(k_blk, HEAD_DIM, axis=1)
        hi, mid, lo = _split3(f_cum[:, hd:hd + 1])
        pieces = lambda base, a, b, c3, rest: (
            jnp.where(lane == base, a,
            jnp.where(lane == base + 1, b,
            jnp.where(lane == base + 2, c3, rest))))
        ones_q = jnp.where(lane < HEAD_DIM + 2 * N_SPLIT, 1.0, 0.0)
        ext_q = pieces(HEAD_DIM, hi, mid, lo, ones_q)
        ext_k = pieces(HEAD_DIM + N_SPLIT, -hi, -mid, -lo, ones_q)
        out = slice(LANES * hd, LANES * (hd + 1))
        qp_ref[0, :, out] = jnp.where(in_head, q_blk, ext_q).astype(BF16)
        kp_ref[0, :, out] = jnp.where(in_head, k_blk, ext_k).astype(BF16)


def _pre_out(bsz, seq):
    out_shape = (
        jax.ShapeDtypeStruct((bsz, seq, HP), BF16),
        jax.ShapeDtypeStruct((bsz, seq, HP), BF16),
        jax.ShapeDtypeStruct((bsz, D_MODEL, seq), BF16),
    )
    out_specs = (
        pl.BlockSpec((1, TS, HP), lambda b, s: (b, s, 0)),
        pl.BlockSpec((1, TS, HP), lambda b, s: (b, s, 0)),
        pl.BlockSpec((1, D_MODEL, TS), lambda b, s: (b, 0, s)),
    )
    return out_shape, out_specs


def _fox_pre_call(x, mod, layer, g, wq, wk, wvt, wf, bf):
    bsz, seq, d = x.shape
    out_shape, out_specs = _pre_out(bsz, seq)
    return pl.pallas_call(
        _fox_pre_kernel,
        out_shape=out_shape,
        grid=(bsz, seq // TS),
        in_specs=[
            pl.BlockSpec((1, TS, d), lambda b, s: (b, s, 0)),
            pl.BlockSpec((1, 1, N_MOD, d), lambda b, s: (layer, b, 0, 0)),
            _const_spec(g.shape),
            _const_spec(wq.shape), _const_spec(wk.shape), _const_spec(wvt.shape),
            _const_spec(wf.shape), _const_spec(bf.shape),
        ],
        out_specs=out_specs,
        scratch_shapes=[pltpu.VMEM((1, HEADS), F32)],
        compiler_params=_params(2),
        name="fox_pre",
    )(x, mod, g, wq, wk, wvt, wf, bf)


def _mla_pre_kernel(x_ref, mod_ref, g_ref, wdq_ref, gq_ref, wuq_ref, wdkv_ref, wdkvr_ref,
                    gkv_ref, wuk_ref, wuvt_ref, cos_ref, sin_ref, qp_ref, kp_ref, vt_ref):
    ts = x_ref.shape[1]
    hb = _modulated_norm(x_ref, mod_ref, g_ref, 0).astype(BF16)
    cos = cos_ref[0]
    sin = sin_ref[0]
    lane = lax.broadcasted_iota(jnp.int32, (ts, LANES), 1)
    first_half = lane < HEAD_DIM + ROPE_HALF
    scale = (HEAD_DIM + ROPE_DIM) ** -0.5

    cq = _rms(_dot(hb, wdq_ref[...]), gq_ref[...]).astype(BF16)
    q = _dot(cq, wuq_ref[...])
    ckv = _rms(_dot(hb, wdkv_ref[...]), gkv_ref[...]).astype(BF16)
    vt_ref[0] = _dot_nt(wuvt_ref[...], ckv).astype(BF16)
    k_nope = _dot(ckv, wuk_ref[...])
    k_rope = _rope_rotate(_dot(hb, wdkvr_ref[...]), cos, sin, first_half)
    for hd in range(HEADS):
        out = slice(LANES * hd, LANES * (hd + 1))
        qp_ref[0, :, out] = (
            _rope_rotate(q[:, out], cos, sin, first_half) * scale).astype(BF16)
        kp_ref[0, :, out] = (k_nope[:, out] + k_rope).astype(BF16)


def _mla_pre_call(x, mod, layer, g, wdq, gq, wuq, wdkv, wdkvr, gkv, wuk, wuvt, tables):
    bsz, seq, d = x.shape
    cos, sin = tables
    out_shape, out_specs = _pre_out(bsz, seq)
    return pl.pallas_call(
        _mla_pre_kernel,
        out_shape=out_shape,
        grid=(bsz, seq // TS),
        in_specs=[
            pl.BlockSpec((1, TS, d), lambda b, s: (b, s, 0)),
            pl.BlockSpec((1, 1, N_MOD, d), lambda b, s: (layer, b, 0, 0)),
            _const_spec(g.shape),
            _const_spec(wdq.shape), _const_spec(gq.shape), _const_spec(wuq.shape),
            _const_spec(wdkv.shape), _const_spec(wdkvr.shape), _const_spec(gkv.shape),
            _const_spec(wuk.shape), _const_spec(wuvt.shape),
            pl.BlockSpec((1, TS, LANES), lambda b, s: (b, s, 0)),
            pl.BlockSpec((1, TS, LANES), lambda b, s: (b, s, 0)),
        ],
        out_specs=out_specs,
        compiler_params=_params(2),
        name="mla_pre",
    )(x, mod, g, wdq, gq, wuq, wdkv, wdkvr, gkv, wuk, wuvt, cos, sin)


def _attn_kernel(q_ref, k_ref, vt_ref, o_ref, *s_refs):
    seq = q_ref.shape[1]
    key = lax.broadcasted_iota(jnp.int32, (TQ, TQ), 0)
    qry = lax.broadcasted_iota(jnp.int32, (TQ, TQ), 1)
    causal = key <= qry
    for qi in range(seq // TQ):
        s_ref = s_refs[qi]
        q0 = TQ * qi
        w = q0 + TQ
        q = q_ref[0, q0:w, :]
        s_diag = jnp.where(causal, _dot_nt(k_ref[0, q0:w, :], q), -jnp.inf)
        s_ref[q0:w, :] = s_diag
        m = jnp.max(s_diag, axis=0, keepdims=True)
        if qi > 0:
            s_full = _dot_nt(k_ref[0, 0:q0, :], q)
            s_ref[0:q0, :] = s_full
            m = jnp.maximum(m, jnp.max(s_full, axis=0, keepdims=True))
        p = jnp.exp(s_ref[...] - m)
        l = jnp.sum(p, axis=0, keepdims=True)
        o_t = _dot(vt_ref[0, :, 0:w], p.astype(BF16))
        o_ref[0, :, q0:w] = (o_t / l).astype(BF16)


def _attn_call(qp, kp, vt):
    bsz, seq, _ = qp.shape
    return pl.pallas_call(
        _attn_kernel,
        out_shape=jax.ShapeDtypeStruct((bsz, D_MODEL, seq), BF16),
        grid=(bsz, HEADS),
        in_specs=[
            pl.BlockSpec((1, seq, LANES), lambda b, h: (b, 0, h)),
            pl.BlockSpec((1, seq, LANES), lambda b, h: (b, 0, h)),
            pl.BlockSpec((1, HEAD_DIM, seq), lambda b, h: (b, h, 0)),
        ],
        out_specs=pl.BlockSpec((1, HEAD_DIM, seq), lambda b, h: (b, h, 0)),
        scratch_shapes=[pltpu.VMEM((TQ * (qi + 1), TQ), F32) for qi in range(seq // TQ)],
        compiler_params=_params(2),
        name="causal_attn",
    )(qp, kp, vt)


FF_CHUNK = 1024


def _post_kernel(ot_ref, x_ref, mod_ref, g_ref, wo_ref, w1_ref, w2_ref, gf_ref, out_ref,
                 *, final):
    g_m = mod_ref[0, 0, 2:3, :]
    sh_f = mod_ref[0, 0, 3:4, :]
    sc_f = mod_ref[0, 0, 4:5, :]
    g_f = mod_ref[0, 0, 5:6, :]
    x1 = x_ref[0] + g_m * _dot_tn(ot_ref[0], wo_ref[...])
    hb = (_rms(x1, g_ref[...]) * (1.0 + sc_f) + sh_f).astype(BF16)
    acc = jnp.zeros_like(x1)
    for j in range(D_FF // FF_CHUNK):
        a = jnp.maximum(_dot(hb, w1_ref[:, FF_CHUNK * j:FF_CHUNK * (j + 1)]), 0.0)
        acc = acc + _dot((a * a).astype(BF16), w2_ref[FF_CHUNK * j:FF_CHUNK * (j + 1), :])
    x2 = x1 + g_f * acc
    if final:
        x2 = _rms(x2, gf_ref[...])
    out_ref[0] = x2


def _post_call(ot, x, mod, layer, g, wo, w1, w2, gf, final):
    bsz, seq, d = x.shape
    return pl.pallas_call(
        functools.partial(_post_kernel, final=final),
        out_shape=jax.ShapeDtypeStruct((bsz, seq, d), F32),
        grid=(bsz, seq // TS),
        in_specs=[
            pl.BlockSpec((1, d, TS), lambda b, s: (b, 0, s)),
            pl.BlockSpec((1, TS, d), lambda b, s: (b, s, 0)),
            pl.BlockSpec((1, 1, N_MOD, d), lambda b, s: (layer, b, 0, 0)),
            _const_spec(g.shape),
            _const_spec(wo.shape), _const_spec(w1.shape), _const_spec(w2.shape),
            _const_spec(gf.shape),
        ],
        out_specs=pl.BlockSpec((1, TS, d), lambda b, s: (b, s, 0)),
        compiler_params=_params(2),
        name="post_mlp",
    )(ot, x, mod, g, wo, w1, w2, gf)


def _pad_heads(w, used):
    k = w.shape[0]
    return jnp.pad(w.reshape(k, HEADS, used),
                   ((0, 0), (0, 0), (0, LANES - used))).reshape(k, HP)


def kernel(x, c, positions, ada_w, ada_b, norm_mix_g, norm_mlp_g, fox_w_in, fox_b_f, fox_w_out, mla_w_dq, mla_q_norm_g, mla_w_uq, mla_w_dkv, mla_kv_norm_g, mla_w_ukv, mla_w_out, mlp_w1, mlp_w2, final_norm_g):
    bsz, seq, d = x.shape
    assert d == D_MODEL and seq % TS == 0 and seq % TQ == 0
    assert ada_w.shape == (DEPTH, d, N_MOD * d)
    assert fox_w_in.shape[1:] == (d, 3 * d + HEADS)
    assert mla_w_uq.shape[1:] == (Q_RANK, HEADS * (HEAD_DIM + ROPE_DIM))
    assert mla_w_dkv.shape[1:] == (d, KV_RANK + ROPE_DIM)
    assert mla_w_ukv.shape[1:] == (KV_RANK, HEADS * 2 * HEAD_DIM)

    mod = _mod_call(c, ada_w, ada_b).reshape(DEPTH, bsz, N_MOD, d)
    tables = _rope_call(positions)
    gf = final_norm_g.reshape(1, d)

    for i in range(DEPTH):
        j = i // 2
        g_mix = norm_mix_g[i].reshape(1, d)
        if i % 2 == 0:
            w_in = fox_w_in[j]
            wq = w_in[:, :d].astype(BF16)
            wk = w_in[:, d:2 * d].astype(BF16)
            wvt = w_in[:, 2 * d:3 * d].T.astype(BF16)
            wf = w_in[:, 3 * d:].astype(BF16)
            bf = fox_b_f[j].reshape(1, HEADS)
            qp, kp, vt = _fox_pre_call(x, mod, i, g_mix, wq, wk, wvt, wf, bf)
            wo = fox_w_out[j].astype(BF16)
        else:
            wdq = mla_w_dq[j].astype(BF16)
            gq = mla_q_norm_g[j].reshape(1, Q_RANK)
            wuq = _pad_heads(mla_w_uq[j], HEAD_DIM + ROPE_DIM).astype(BF16)
            wdkv = mla_w_dkv[j][:, :KV_RANK].astype(BF16)
            wdkvr = jnp.pad(mla_w_dkv[j][:, KV_RANK:],
                            ((0, 0), (HEAD_DIM, LANES - HEAD_DIM - ROPE_DIM))).astype(BF16)
            gkv = mla_kv_norm_g[j].reshape(1, KV_RANK)
            w_ukv = mla_w_ukv[j].reshape(KV_RANK, HEADS, 2 * HEAD_DIM)
            wuk = _pad_heads(w_ukv[:, :, :HEAD_DIM].reshape(KV_RANK, HEADS * HEAD_DIM),
                             HEAD_DIM).astype(BF16)
            wuvt = w_ukv[:, :, HEAD_DIM:].reshape(KV_RANK, HEADS * HEAD_DIM).T.astype(BF16)
            qp, kp, vt = _mla_pre_call(x, mod, i, g_mix, wdq, gq, wuq, wdkv, wdkvr, gkv,
                                       wuk, wuvt, tables)
            wo = mla_w_out[j].astype(BF16)
        ot = _attn_call(qp, kp, vt)
        x = _post_call(ot, x, mod, i, norm_mlp_g[i].reshape(1, d), wo,
                       mlp_w1[i].astype(BF16), mlp_w2[i].astype(BF16), gf,
                       final=(i == DEPTH - 1))
    return x
```

```python
import functools

import jax
import jax.numpy as jnp
from jax import lax
from jax.experimental import pallas as pl
from jax.experimental.pallas import tpu as pltpu

D_MODEL = 1024
DEPTH = 4
HEADS = 16
HEAD_DIM = 64
ROPE_DIM = 32
ROPE_HALF = ROPE_DIM // 2
Q_RANK = 256
KV_RANK = 128
D_FF = 4 * D_MODEL
ROPE_THETA = 10000.0
NORM_EPS = 1e-6
N_MOD = 6

LANES = 128
HP = HEADS * LANES
N_SPLIT = 3

V_ROWS = HEAD_DIM + 16
LOG2E = 1.4426950408889634

TS = 512
TQ = 512
ATTN_HEADS = 4
LOGITS_AHEAD = 1
VMEM_LIMIT = 56 * 1024 * 1024

BF16 = jnp.bfloat16
F32 = jnp.float32
NT_DIMS = (((1,), (1,)), ((), ()))
TN_DIMS = (((0,), (0,)), ((), ()))


def _dot(a, b):
    return jnp.dot(a, b, preferred_element_type=F32)


def _dot_nt(a, b):
    return lax.dot_general(a, b, NT_DIMS, preferred_element_type=F32)


def _dot_tn(a, b):
    return lax.dot_general(a, b, TN_DIMS, preferred_element_type=F32)


def _dot_exact(a, b):
    return jnp.dot(a, b, preferred_element_type=F32, precision=lax.Precision.HIGHEST)


def _rms(x, g):
    ms = jnp.mean(x * x, axis=-1, keepdims=True)
    return x * lax.rsqrt(ms + NORM_EPS) * g


def _const_spec(shape):
    zeros = (0,) * len(shape)
    return pl.BlockSpec(shape, lambda *_: zeros, pipeline_mode=pl.Buffered(1))


def _params(n_axes):
    return pltpu.CompilerParams(
        dimension_semantics=("arbitrary",) * n_axes,
        vmem_limit_bytes=VMEM_LIMIT)


MOD_TN = 1024


def _mod_kernel(c_ref, w_ref, b_ref, o_ref):
    ca = jax.nn.silu(c_ref[...])
    o_ref[0] = _dot_exact(ca, w_ref[0]) + b_ref[0]


def _mod_call(c, ada_w, ada_b):
    depth, d, n = ada_w.shape
    bsz = c.shape[0]
    return pl.pallas_call(
        _mod_kernel,
        out_shape=jax.ShapeDtypeStruct((depth, bsz, n), F32),
        grid=(depth, n // MOD_TN),
        in_specs=[
            pl.BlockSpec((bsz, d), lambda l, j: (0, 0)),
            pl.BlockSpec((1, d, MOD_TN), lambda l, j: (l, 0, j)),
            pl.BlockSpec((1, 1, MOD_TN), lambda l, j: (l, 0, j)),
        ],
        out_specs=pl.BlockSpec((1, bsz, MOD_TN), lambda l, j: (l, 0, j)),
        compiler_params=_params(2),
        name="adaln_mod",
    )(c, ada_w, ada_b.reshape(depth, 1, n))


def _rope_kernel(pos_ref, c_ref, s_ref):
    seq = pos_ref.shape[1]
    lane = lax.broadcasted_iota(jnp.int32, (1, LANES), 1)
    is_rope = (lane >= HEAD_DIM) & (lane < HEAD_DIM + ROPE_DIM)
    fidx = ((lane - HEAD_DIM) % ROPE_HALF).astype(F32)
    inv_freq = ROPE_THETA ** (-(2.0 * fidx) / ROPE_DIM)
    ang = pos_ref[0].astype(F32) * inv_freq
    rope_mask = jnp.broadcast_to(is_rope, (seq, LANES))
    pass_mask = jnp.broadcast_to(lane < HEAD_DIM, (seq, LANES))
    c_ref[0] = jnp.where(rope_mask, jnp.cos(ang), jnp.where(pass_mask, 1.0, 0.0))
    s_ref[0] = jnp.where(rope_mask, jnp.sin(ang), 0.0)


def _rope_call(positions):
    bsz, seq = positions.shape
    return pl.pallas_call(
        _rope_kernel,
        out_shape=(
            jax.ShapeDtypeStruct((bsz, seq, LANES), F32),
            jax.ShapeDtypeStruct((bsz, seq, LANES), F32),
        ),
        grid=(bsz,),
        in_specs=[pl.BlockSpec((1, seq, 1), lambda b: (b, 0, 0))],
        out_specs=(
            pl.BlockSpec((1, seq, LANES), lambda b: (b, 0, 0)),
            pl.BlockSpec((1, seq, LANES), lambda b: (b, 0, 0)),
        ),
        compiler_params=_params(1),
        name="rope_tables",
    )(positions.reshape(bsz, seq, 1))


def _modulated_norm(x_ref, mod_ref, g_ref, shift_row):
    x = x_ref[0]
    sh = mod_ref[0, 0, shift_row:shift_row + 1, :]
    sc = mod_ref[0, 0, shift_row + 1:shift_row + 2, :]
    return _rms(x, g_ref[...]) * (1.0 + sc) + sh


def _split3(x):
    hi = x.astype(BF16)
    r1 = x - hi.astype(F32)
    mid = r1.astype(BF16)
    lo = (r1 - mid.astype(F32)).astype(BF16)
    return hi, mid, lo


def _log_sigmoid(x):
    return jnp.minimum(x, 0.0) - jnp.log1p(jnp.exp(-jnp.abs(x)))


def _rope_rotate(blk, cos, sin, first_half):
    rot = jnp.where(first_half,
                    -pltpu.roll(blk, LANES - ROPE_HALF, axis=1),
                    pltpu.roll(blk, ROPE_HALF, axis=1))
    return blk * cos + rot * sin


def _store_vt(vt_ref, v_t):
    ts = v_t.shape[1]
    sub = lax.broadcasted_iota(jnp.int32, (V_ROWS - HEAD_DIM, ts), 0)
    ones_rows = jnp.where(sub == 0, 1.0, 0.0).astype(BF16)
    for hd in range(HEADS):
        r0 = V_ROWS * hd
        vt_ref[0, r0:r0 + HEAD_DIM, :] = v_t[HEAD_DIM * hd:HEAD_DIM * (hd + 1), :].astype(BF16)
        vt_ref[0, r0 + HEAD_DIM:r0 + V_ROWS, :] = ones_rows


def _fox_pre_kernel(x_ref, mod_ref, g_ref, wq_ref, wk_ref, wvt_ref, wf_ref, bf_ref,
                    qp_ref, kp_ref, vt_ref, carry_ref):
    ts = x_ref.shape[1]

    @pl.when(pl.program_id(1) == 0)
    def _():
        carry_ref[...] = jnp.zeros_like(carry_ref)

    hb = _modulated_norm(x_ref, mod_ref, g_ref, 0).astype(BF16)
    q = _dot(hb, wq_ref[...]) * (HEAD_DIM ** -0.5 * LOG2E)
    k = _dot(hb, wk_ref[...])
    _store_vt(vt_ref, _dot_nt(wvt_ref[...], hb))

    lf = _log_sigmoid(_dot(hb, wf_ref[...]) + bf_ref[...])
    r = lax.broadcasted_iota(jnp.int32, (ts, ts), 0)
    c = lax.broadcasted_iota(jnp.int32, (ts, ts), 1)
    lower = jnp.where(c <= r, 1.0, 0.0).astype(BF16)
    f_cum = carry_ref[...]
    for piece in _split3(lf):
        f_cum = f_cum + _dot(lower, piece)
    carry_ref[...] = f_cum[ts - 1:ts, :]

    hsel = lax.broadcasted_iota(jnp.int32, (HEADS, LANES), 0)
    lsel = lax.broadcasted_iota(jnp.int32, (HEADS, LANES), 1) - 2 * N_SPLIT * hsel
    g = jnp.zeros((ts, LANES), F32)
    for j, piece in enumerate(_split3(f_cum * LOG2E)):
        sel = jnp.where(lsel == j, 1.0, jnp.where(lsel == j + N_SPLIT, -1.0, 0.0))
        g = g + _dot(piece, sel.astype(BF16))

    lane = lax.broadcasted_iota(jnp.int32, (ts, LANES), 1)
    in_head = lane < HEAD_DIM
    in_pos = lane < HEAD_DIM + N_SPLIT
    in_ext = lane < HEAD_DIM + 2 * N_SPLIT
    for hd in range(HEADS):
        cols = slice(LANES * (hd // 2), LANES * (hd // 2 + 1))
        q_blk = q[:, cols]
        k_blk = k[:, cols]
        if hd % 2 == 1:
            q_blk = pltpu.roll(q_blk, HEAD_DIM, axis=1)
            k_blk = pltpu.roll(k_blk, HEAD_DIM, axis=1)
        rot = pltpu.roll(g, (HEAD_DIM - 2 * N_SPLIT * hd) % LANES, axis=1)
        ext_q = jnp.where(in_pos, rot, jnp.where(in_ext, 1.0, 0.0))
        ext_k = jnp.where(in_pos, 1.0, jnp.where(in_ext, rot, 0.0))
        out = slice(LANES * hd, LANES * (hd + 1))
        qp_ref[0, :, out] = jnp.where(in_head, q_blk, ext_q).astype(BF16)
        kp_ref[0, :, out] = jnp.where(in_head, k_blk, ext_k).astype(BF16)


def _pre_out(bsz, seq):
    out_shape = (
        jax.ShapeDtypeStruct((bsz, seq, HP), BF16),
        jax.ShapeDtypeStruct((bsz, seq, HP), BF16),
        jax.ShapeDtypeStruct((bsz, HEADS * V_ROWS, seq), BF16),
    )
    out_specs = (
        pl.BlockSpec((1, TS, HP), lambda b, s: (b, s, 0)),
        pl.BlockSpec((1, TS, HP), lambda b, s: (b, s, 0)),
        pl.BlockSpec((1, HEADS * V_ROWS, TS), lambda b, s: (b, 0, s)),
    )
    return out_shape, out_specs


def _fox_pre_call(x, mod, layer, g, wq, wk, wvt, wf, bf):
    bsz, seq, d = x.shape
    out_shape, out_specs = _pre_out(bsz, seq)
    return pl.pallas_call(
        _fox_pre_kernel,
        out_shape=out_shape,
        grid=(bsz, seq // TS),
        in_specs=[
            pl.BlockSpec((1, TS, d), lambda b, s: (b, s, 0)),
            pl.BlockSpec((1, 1, N_MOD, d), lambda b, s: (layer, b, 0, 0)),
            _const_spec(g.shape),
            _const_spec(wq.shape), _const_spec(wk.shape), _const_spec(wvt.shape),
            _const_spec(wf.shape), _const_spec(bf.shape),
        ],
        out_specs=out_specs,
        scratch_shapes=[pltpu.VMEM((1, HEADS), F32)],
        compiler_params=_params(2),
        name="fox_pre",
    )(x, mod, g, wq, wk, wvt, wf, bf)


def _mla_pre_kernel(x_ref, mod_ref, g_ref, wdq_ref, gq_ref, wuq_ref, wuqr_ref, wdkv_ref, wdkvr_ref,
                    gkv_ref, wuk_ref, wuvt_ref, cos_ref, sin_ref, qp_ref, kp_ref, vt_ref):
    ts = x_ref.shape[1]
    hb = _modulated_norm(x_ref, mod_ref, g_ref, 0).astype(BF16)
    cos = cos_ref[0]
    sin = sin_ref[0]
    lane = lax.broadcasted_iota(jnp.int32, (ts, LANES), 1)
    first_half = lane < HEAD_DIM + ROPE_HALF
    scale = (HEAD_DIM + ROPE_DIM) ** -0.5 * LOG2E

    cq = _rms(_dot(hb, wdq_ref[...]), gq_ref[...]).astype(BF16)
    q = _dot(cq, wuq_ref[...])
    q_rot = _dot(cq, wuqr_ref[...])
    ckv = _rms(_dot(hb, wdkv_ref[...]), gkv_ref[...]).astype(BF16)
    _store_vt(vt_ref, _dot_nt(wuvt_ref[...], ckv))
    k_nope = _dot(ckv, wuk_ref[...])
    k_rope = _rope_rotate(_dot(hb, wdkvr_ref[...]), cos, sin, first_half)
    for hd in range(HEADS):
        out = slice(LANES * hd, LANES * (hd + 1))
        qp_ref[0, :, out] = ((q[:, out] * cos + q_rot[:, out] * sin) * scale).astype(BF16)
        kp_ref[0, :, out] = (k_nope[:, out] + k_rope).astype(BF16)


def _mla_pre_call(x, mod, layer, g, wdq, gq, wuq, wuqr, wdkv, wdkvr, gkv, wuk, wuvt, tables):
    bsz, seq, d = x.shape
    cos, sin = tables
    out_shape, out_specs = _pre_out(bsz, seq)
    return pl.pallas_call(
        _mla_pre_kernel,
        out_shape=out_shape,
        grid=(bsz, seq // TS),
        in_specs=[
            pl.BlockSpec((1, TS, d), lambda b, s: (b, s, 0)),
            pl.BlockSpec((1, 1, N_MOD, d), lambda b, s: (layer, b, 0, 0)),
            _const_spec(g.shape),
            _const_spec(wdq.shape), _const_spec(gq.shape), _const_spec(wuq.shape),
            _const_spec(wuqr.shape),
            _const_spec(wdkv.shape), _const_spec(wdkvr.shape), _const_spec(gkv.shape),
            _const_spec(wuk.shape), _const_spec(wuvt.shape),
            pl.BlockSpec((1, TS, LANES), lambda b, s: (b, s, 0)),
            pl.BlockSpec((1, TS, LANES), lambda b, s: (b, s, 0)),
        ],
        out_specs=out_specs,
        compiler_params=_params(2),
        name="mla_pre",
    )(x, mod, g, wdq, gq, wuq, wuqr, wdkv, wdkvr, gkv, wuk, wuvt, cos, sin)


def _attn_kernel(q_ref, k_ref, vt_ref, o_ref):
    seq = q_ref.shape[1]
    key = lax.broadcasted_iota(jnp.int32, (TQ, TQ), 0)
    qry = lax.broadcasted_iota(jnp.int32, (TQ, TQ), 1)
    causal = key <= qry
    blocks = [(h, qi) for h in range(ATTN_HEADS) for qi in range(seq // TQ)]

    def logits(h, qi):
        w = TQ * (qi + 1)
        q = q_ref[0, w - TQ:w, LANES * h:LANES * (h + 1)]
        return _dot_nt(k_ref[0, 0:w, LANES * h:LANES * (h + 1)], q)

    def finish(h, qi, s):
        q0 = TQ * qi
        w = q0 + TQ
        s_diag = jnp.where(causal, s[q0:w], -jnp.inf)
        m = jnp.max(s_diag, axis=0, keepdims=True)
        if qi > 0:
            m = jnp.maximum(m, jnp.max(s[0:q0], axis=0, keepdims=True))
        p = jnp.exp2((s_diag - m).astype(BF16))
        if qi > 0:
            p = jnp.concatenate([jnp.exp2((s[0:q0] - m).astype(BF16)), p], axis=0)
        acc = _dot(vt_ref[0, V_ROWS * h:V_ROWS * (h + 1), 0:w], p)
        o_ref[0, HEAD_DIM * h:HEAD_DIM * (h + 1), q0:w] = (
            acc[:HEAD_DIM] / acc[HEAD_DIM:HEAD_DIM + 1]).astype(BF16)

    ahead = [logits(*blk) for blk in blocks[:LOGITS_AHEAD]]
    for i, blk in enumerate(blocks):
        if i + LOGITS_AHEAD < len(blocks):
            ahead.append(logits(*blocks[i + LOGITS_AHEAD]))
        finish(*blk, ahead.pop(0))


def _attn_call(qp, kp, vt):
    bsz, seq, _ = qp.shape
    return pl.pallas_call(
        _attn_kernel,
        out_shape=jax.ShapeDtypeStruct((bsz, D_MODEL, seq), BF16),
        grid=(bsz, HEADS // ATTN_HEADS),
        in_specs=[
            pl.BlockSpec((1, seq, ATTN_HEADS * LANES), lambda b, h: (b, 0, h)),
            pl.BlockSpec((1, seq, ATTN_HEADS * LANES), lambda b, h: (b, 0, h)),
            pl.BlockSpec((1, ATTN_HEADS * V_ROWS, seq), lambda b, h: (b, h, 0)),
        ],
        out_specs=pl.BlockSpec((1, ATTN_HEADS * HEAD_DIM, seq), lambda b, h: (b, h, 0)),
        compiler_params=_params(2),
        name="causal_attn",
    )(qp, kp, vt)


FF_CHUNK = 1024


def _post_kernel(ot_ref, x_ref, mod_ref, g_ref, wo_ref, w1_ref, w2_ref, gf_ref, out_ref,
                 *, final):
    g_m = mod_ref[0, 0, 2:3, :]
    sh_f = mod_ref[0, 0, 3:4, :]
    sc_f = mod_ref[0, 0, 4:5, :]
    g_f = mod_ref[0, 0, 5:6, :]
    x1 = x_ref[0] + g_m * _dot_tn(ot_ref[0], wo_ref[...])
    hb = (_rms(x1, g_ref[...]) * (1.0 + sc_f) + sh_f).astype(BF16)
    acc = jnp.zeros_like(x1)
    for j in range(D_FF // FF_CHUNK):
        a = jnp.maximum(_dot(hb, w1_ref[:, FF_CHUNK * j:FF_CHUNK * (j + 1)]), 0.0)
        acc = acc + _dot((a * a).astype(BF16), w2_ref[FF_CHUNK * j:FF_CHUNK * (j + 1), :])
    x2 = x1 + g_f * acc
    if final:
        x2 = _rms(x2, gf_ref[...])
    out_ref[0] = x2


def _post_call(ot, x, mod, layer, g, wo, w1, w2, gf, final):
    bsz, seq, d = x.shape
    return pl.pallas_call(
        functools.partial(_post_kernel, final=final),
        out_shape=jax.ShapeDtypeStruct((bsz, seq, d), F32),
        grid=(bsz, seq // TS),
        in_specs=[
            pl.BlockSpec((1, d, TS), lambda b, s: (b, 0, s)),
            pl.BlockSpec((1, TS, d), lambda b, s: (b, s, 0)),
            pl.BlockSpec((1, 1, N_MOD, d), lambda b, s: (layer, b, 0, 0)),
            _const_spec(g.shape),
            _const_spec(wo.shape), _const_spec(w1.shape), _const_spec(w2.shape),
            _const_spec(gf.shape),
        ],
        out_specs=pl.BlockSpec((1, TS, d), lambda b, s: (b, s, 0)),
        compiler_params=_params(2),
        name="post_mlp",
    )(ot, x, mod, g, wo, w1, w2, gf)


def _pad_heads(w, used):
    k = w.shape[0]
    return jnp.pad(w.reshape(k, HEADS, used),
                   ((0, 0), (0, 0), (0, LANES - used))).reshape(k, HP)


def kernel(x, c, positions, ada_w, ada_b, norm_mix_g, norm_mlp_g, fox_w_in, fox_b_f, fox_w_out, mla_w_dq, mla_q_norm_g, mla_w_uq, mla_w_dkv, mla_kv_norm_g, mla_w_ukv, mla_w_out, mlp_w1, mlp_w2, final_norm_g):
    bsz, seq, d = x.shape
    assert d == D_MODEL and seq % TS == 0 and seq % TQ == 0
    assert ada_w.shape == (DEPTH, d, N_MOD * d)
    assert fox_w_in.shape[1:] == (d, 3 * d + HEADS)
    assert mla_w_uq.shape[1:] == (Q_RANK, HEADS * (HEAD_DIM + ROPE_DIM))
    assert mla_w_dkv.shape[1:] == (d, KV_RANK + ROPE_DIM)
    assert mla_w_ukv.shape[1:] == (KV_RANK, HEADS * 2 * HEAD_DIM)

    mod = _mod_call(c, ada_w, ada_b).reshape(DEPTH, bsz, N_MOD, d)
    tables = _rope_call(positions)
    gf = final_norm_g.reshape(1, d)

    for i in range(DEPTH):
        j = i // 2
        g_mix = norm_mix_g[i].reshape(1, d)
        if i % 2 == 0:
            w_in = fox_w_in[j]
            wq = w_in[:, :d].astype(BF16)
            wk = w_in[:, d:2 * d].astype(BF16)
            wvt = w_in[:, 2 * d:3 * d].T.astype(BF16)
            wf = w_in[:, 3 * d:].astype(BF16)
            bf = fox_b_f[j].reshape(1, HEADS)
            qp, kp, vt = _fox_pre_call(x, mod, i, g_mix, wq, wk, wvt, wf, bf)
            wo = fox_w_out[j].astype(BF16)
        else:
            wdq = mla_w_dq[j].astype(BF16)
            gq = mla_q_norm_g[j].reshape(1, Q_RANK)
            w_uq = mla_w_uq[j].reshape(Q_RANK, HEADS, HEAD_DIM + ROPE_DIM)
            wuq = _pad_heads(mla_w_uq[j], HEAD_DIM + ROPE_DIM).astype(BF16)
            wuqr = jnp.pad(
                jnp.concatenate([-w_uq[:, :, HEAD_DIM + ROPE_HALF:],
                                 w_uq[:, :, HEAD_DIM:HEAD_DIM + ROPE_HALF]], axis=2),
                ((0, 0), (0, 0), (HEAD_DIM, LANES - HEAD_DIM - ROPE_DIM)),
            ).reshape(Q_RANK, HP).astype(BF16)
            wdkv = mla_w_dkv[j][:, :KV_RANK].astype(BF16)
            wdkvr = jnp.pad(mla_w_dkv[j][:, KV_RANK:],
                            ((0, 0), (HEAD_DIM, LANES - HEAD_DIM - ROPE_DIM))).astype(BF16)
            gkv = mla_kv_norm_g[j].reshape(1, KV_RANK)
            w_ukv = mla_w_ukv[j].reshape(KV_RANK, HEADS, 2 * HEAD_DIM)
            wuk = _pad_heads(w_ukv[:, :, :HEAD_DIM].reshape(KV_RANK, HEADS * HEAD_DIM),
                             HEAD_DIM).astype(BF16)
            wuvt = w_ukv[:, :, HEAD_DIM:].reshape(KV_RANK, HEADS * HEAD_DIM).T.astype(BF16)
            qp, kp, vt = _mla_pre_call(x, mod, i, g_mix, wdq, gq, wuq, wuqr, wdkv, wdkvr, gkv,
                                       wuk, wuvt, tables)
            wo = mla_w_out[j].astype(BF16)
        ot = _attn_call(qp, kp, vt)
        x = _post_call(ot, x, mod, i, norm_mlp_g[i].reshape(1, d), wo,
                       mlp_w1[i].astype(BF16), mlp_w2[i].astype(BF16), gf,
                       final=(i == DEPTH - 1))
    return x
```

```python
import functools

import jax
import jax.numpy as jnp
from jax import lax
from jax.experimental import pallas as pl
from jax.experimental.pallas import tpu as pltpu

D_MODEL = 1024
DEPTH = 4
HEADS = 16
HEAD_DIM = 64
ROPE_DIM = 32
ROPE_HALF = ROPE_DIM // 2
Q_RANK = 256
KV_RANK = 128
D_FF = 4 * D_MODEL
ROPE_THETA = 10000.0
NORM_EPS = 1e-6
N_MOD = 6

LANES = 128
MXU_TILE = 256
HP = HEADS * LANES
N_SPLIT = 3

V_ROWS = HEAD_DIM + 16
LOG2E = 1.4426950408889634

TS = 512
TQ = 512
ATTN_HEADS = 4
LOGITS_AHEAD = 1
VMEM_LIMIT = 56 * 1024 * 1024

BF16 = jnp.bfloat16
F32 = jnp.float32
NT_DIMS = (((1,), (1,)), ((), ()))
TN_DIMS = (((0,), (0,)), ((), ()))


def _dot(a, b):
    return jnp.dot(a, b, preferred_element_type=F32)


def _dot_nt(a, b):
    return lax.dot_general(a, b, NT_DIMS, preferred_element_type=F32)


def _dot_tn(a, b):
    return lax.dot_general(a, b, TN_DIMS, preferred_element_type=F32)


def _dot_exact(a, b):
    return jnp.dot(a, b, preferred_element_type=F32, precision=lax.Precision.HIGHEST)


def _rms(x, g):
    ms = jnp.mean(x * x, axis=-1, keepdims=True)
    return x * lax.rsqrt(ms + NORM_EPS) * g


def _const_spec(shape):
    zeros = (0,) * len(shape)
    return pl.BlockSpec(shape, lambda *_: zeros, pipeline_mode=pl.Buffered(1))


def _params(n_axes):
    return pltpu.CompilerParams(
        dimension_semantics=("arbitrary",) * n_axes,
        vmem_limit_bytes=VMEM_LIMIT)


MOD_TN = 1024


def _mod_kernel(c_ref, w_ref, b_ref, o_ref):
    ca = jax.nn.silu(c_ref[...])
    o_ref[0] = _dot_exact(ca, w_ref[0]) + b_ref[0]


def _mod_call(c, ada_w, ada_b):
    depth, d, n = ada_w.shape
    bsz = c.shape[0]
    return pl.pallas_call(
        _mod_kernel,
        out_shape=jax.ShapeDtypeStruct((depth, bsz, n), F32),
        grid=(depth, n // MOD_TN),
        in_specs=[
            pl.BlockSpec((bsz, d), lambda l, j: (0, 0)),
            pl.BlockSpec((1, d, MOD_TN), lambda l, j: (l, 0, j)),
            pl.BlockSpec((1, 1, MOD_TN), lambda l, j: (l, 0, j)),
        ],
        out_specs=pl.BlockSpec((1, bsz, MOD_TN), lambda l, j: (l, 0, j)),
        compiler_params=_params(2),
        name="adaln_mod",
    )(c, ada_w, ada_b.reshape(depth, 1, n))


def _rope_kernel(pos_ref, c_ref, s_ref):
    seq = pos_ref.shape[1]
    lane = lax.broadcasted_iota(jnp.int32, (1, LANES), 1)
    is_rope = (lane >= HEAD_DIM) & (lane < HEAD_DIM + ROPE_DIM)
    fidx = ((lane - HEAD_DIM) % ROPE_HALF).astype(F32)
    inv_freq = ROPE_THETA ** (-(2.0 * fidx) / ROPE_DIM)
    ang = pos_ref[0].astype(F32) * inv_freq
    rope_mask = jnp.broadcast_to(is_rope, (seq, LANES))
    pass_mask = jnp.broadcast_to(lane < HEAD_DIM, (seq, LANES))
    c_ref[0] = jnp.where(rope_mask, jnp.cos(ang), jnp.where(pass_mask, 1.0, 0.0))
    s_ref[0] = jnp.where(rope_mask, jnp.sin(ang), 0.0)


def _rope_call(positions):
    bsz, seq = positions.shape
    return pl.pallas_call(
        _rope_kernel,
        out_shape=(
            jax.ShapeDtypeStruct((bsz, seq, LANES), F32),
            jax.ShapeDtypeStruct((bsz, seq, LANES), F32),
        ),
        grid=(bsz,),
        in_specs=[pl.BlockSpec((1, seq, 1), lambda b: (b, 0, 0))],
        out_specs=(
            pl.BlockSpec((1, seq, LANES), lambda b: (b, 0, 0)),
            pl.BlockSpec((1, seq, LANES), lambda b: (b, 0, 0)),
        ),
        compiler_params=_params(1),
        name="rope_tables",
    )(positions.reshape(bsz, seq, 1))


def _modulated_norm(x_ref, mod_ref, g_ref, shift_row):
    x = x_ref[0]
    sh = mod_ref[0, 0, shift_row:shift_row + 1, :]
    sc = mod_ref[0, 0, shift_row + 1:shift_row + 2, :]
    return _rms(x, g_ref[...]) * (1.0 + sc) + sh


def _split3(x):
    hi = x.astype(BF16)
    r1 = x - hi.astype(F32)
    mid = r1.astype(BF16)
    lo = (r1 - mid.astype(F32)).astype(BF16)
    return hi, mid, lo


def _log_sigmoid(x):
    return jnp.minimum(x, 0.0) - jnp.log1p(jnp.exp(-jnp.abs(x)))


def _rope_rotate(blk, cos, sin, first_half):
    rot = jnp.where(first_half,
                    -pltpu.roll(blk, LANES - ROPE_HALF, axis=1),
                    pltpu.roll(blk, ROPE_HALF, axis=1))
    return blk * cos + rot * sin


def _store_vt(vt_ref, v_t):
    ts = v_t.shape[1]
    sub = lax.broadcasted_iota(jnp.int32, (V_ROWS - HEAD_DIM, ts), 0)
    ones_rows = jnp.where(sub == 0, 1.0, 0.0).astype(BF16)
    for hd in range(HEADS):
        r0 = V_ROWS * hd
        vt_ref[0, r0:r0 + HEAD_DIM, :] = v_t[HEAD_DIM * hd:HEAD_DIM * (hd + 1), :].astype(BF16)
        vt_ref[0, r0 + HEAD_DIM:r0 + V_ROWS, :] = ones_rows


def _fox_pre_kernel(x_ref, mod_ref, g_ref, wqf_ref, wk_ref, wvt_ref, bf_ref,
                    qp_ref, kp_ref, vt_ref, carry_ref):
    ts = x_ref.shape[1]

    @pl.when(pl.program_id(1) == 0)
    def _():
        carry_ref[...] = jnp.zeros_like(carry_ref)

    hb = _modulated_norm(x_ref, mod_ref, g_ref, 0).astype(BF16)
    d = hb.shape[1]
    qf = _dot(hb, wqf_ref[...])
    k = _dot(hb, wk_ref[...])
    q = qf[:, :d] * (HEAD_DIM ** -0.5 * LOG2E)

    lf = _log_sigmoid(qf[:, d:d + HEADS] + bf_ref[...])
    r = lax.broadcasted_iota(jnp.int32, (ts, ts), 0)
    c = lax.broadcasted_iota(jnp.int32, (ts, ts), 1)
    lower = jnp.where(c <= r, 1.0, 0.0).astype(BF16)
    sums = _dot(lower, jnp.concatenate(_split3(lf), axis=1))
    f_cum = carry_ref[...]
    for j in range(N_SPLIT):
        f_cum = f_cum + sums[:, HEADS * j:HEADS * (j + 1)]
    carry_ref[...] = f_cum[ts - 1:ts, :]

    row = lax.broadcasted_iota(jnp.int32, (N_SPLIT * HEADS, LANES), 0)
    lsel = (lax.broadcasted_iota(jnp.int32, (N_SPLIT * HEADS, LANES), 1)
            - 2 * N_SPLIT * (row % HEADS) - row // HEADS)
    sel = jnp.where(lsel == 0, 1.0, jnp.where(lsel == N_SPLIT, -1.0, 0.0)).astype(BF16)
    g = _dot(jnp.concatenate(_split3(f_cum * LOG2E), axis=1), sel)
    v_t = _dot_nt(wvt_ref[...], hb)

    lane = lax.broadcasted_iota(jnp.int32, (ts, LANES), 1)
    in_head = lane < HEAD_DIM
    in_pos = lane < HEAD_DIM + N_SPLIT
    in_ext = lane < HEAD_DIM + 2 * N_SPLIT

    def store_heads(out_ref, proj, is_q):
        for hd in range(HEADS):
            blk = proj[:, LANES * (hd // 2):LANES * (hd // 2 + 1)]
            if hd % 2 == 1:
                blk = pltpu.roll(blk, HEAD_DIM, axis=1)
            rot = pltpu.roll(g, (HEAD_DIM - 2 * N_SPLIT * hd) % LANES, axis=1)
            if is_q:
                ext = jnp.where(in_pos, rot, jnp.where(in_ext, 1.0, 0.0))
            else:
                ext = jnp.where(in_pos, 1.0, jnp.where(in_ext, rot, 0.0))
            out_ref[0, :, LANES * hd:LANES * (hd + 1)] = (
                jnp.where(in_head, blk, ext).astype(BF16))

    store_heads(qp_ref, q, True)
    store_heads(kp_ref, k, False)
    _store_vt(vt_ref, v_t)


def _pre_out(bsz, seq):
    out_shape = (
        jax.ShapeDtypeStruct((bsz, seq, HP), BF16),
        jax.ShapeDtypeStruct((bsz, seq, HP), BF16),
        jax.ShapeDtypeStruct((bsz, HEADS * V_ROWS, seq), BF16),
    )
    out_specs = (
        pl.BlockSpec((1, TS, HP), lambda b, s: (b, s, 0)),
        pl.BlockSpec((1, TS, HP), lambda b, s: (b, s, 0)),
        pl.BlockSpec((1, HEADS * V_ROWS, TS), lambda b, s: (b, 0, s)),
    )
    return out_shape, out_specs


def _fox_pre_call(x, mod, layer, g, wqf, wk, wvt, bf):
    bsz, seq, d = x.shape
    out_shape, out_specs = _pre_out(bsz, seq)
    return pl.pallas_call(
        _fox_pre_kernel,
        out_shape=out_shape,
        grid=(bsz, seq // TS),
        in_specs=[
            pl.BlockSpec((1, TS, d), lambda b, s: (b, s, 0)),
            pl.BlockSpec((1, 1, N_MOD, d), lambda b, s: (layer, b, 0, 0)),
            _const_spec(g.shape),
            _const_spec(wqf.shape), _const_spec(wk.shape), _const_spec(wvt.shape),
            _const_spec(bf.shape),
        ],
        out_specs=out_specs,
        scratch_shapes=[pltpu.VMEM((1, HEADS), F32)],
        compiler_params=_params(2),
        name="fox_pre",
    )(x, mod, g, wqf, wk, wvt, bf)


def _mla_pre_kernel(x_ref, mod_ref, g_ref, wdq_ref, gq_ref, wuq_ref, wuqr_ref, wdkv_ref, wdkvr_ref,
                    gkv_ref, wuk_ref, wuvt_ref, cos_ref, sin_ref, qp_ref, kp_ref, vt_ref):
    ts = x_ref.shape[1]
    hb = _modulated_norm(x_ref, mod_ref, g_ref, 0).astype(BF16)
    cos = cos_ref[0]
    sin = sin_ref[0]
    lane = lax.broadcasted_iota(jnp.int32, (ts, LANES), 1)
    first_half = lane < HEAD_DIM + ROPE_HALF
    scale = (HEAD_DIM + ROPE_DIM) ** -0.5 * LOG2E

    cq = _rms(_dot(hb, wdq_ref[...]), gq_ref[...]).astype(BF16)
    q = _dot(cq, wuq_ref[...])
    q_rot = _dot(cq, wuqr_ref[...])
    ckv = _rms(_dot(hb, wdkv_ref[...]), gkv_ref[...]).astype(BF16)
    _store_vt(vt_ref, _dot_nt(wuvt_ref[...], ckv))
    k_nope = _dot(ckv, wuk_ref[...])
    k_rope = _rope_rotate(_dot(hb, wdkvr_ref[...]), cos, sin, first_half)
    for hd in range(HEADS):
        out = slice(LANES * hd, LANES * (hd + 1))
        qp_ref[0, :, out] = ((q[:, out] * cos + q_rot[:, out] * sin) * scale).astype(BF16)
        kp_ref[0, :, out] = (k_nope[:, out] + k_rope).astype(BF16)


def _mla_pre_call(x, mod, layer, g, wdq, gq, wuq, wuqr, wdkv, wdkvr, gkv, wuk, wuvt, tables):
    bsz, seq, d = x.shape
    cos, sin = tables
    out_shape, out_specs = _pre_out(bsz, seq)
    return pl.pallas_call(
        _mla_pre_kernel,
        out_shape=out_shape,
        grid=(bsz, seq // TS),
        in_specs=[
            pl.BlockSpec((1, TS, d), lambda b, s: (b, s, 0)),
            pl.BlockSpec((1, 1, N_MOD, d), lambda b, s: (layer, b, 0, 0)),
            _const_spec(g.shape),
            _const_spec(wdq.shape), _const_spec(gq.shape), _const_spec(wuq.shape),
            _const_spec(wuqr.shape),
            _const_spec(wdkv.shape), _const_spec(wdkvr.shape), _const_spec(gkv.shape),
            _const_spec(wuk.shape), _const_spec(wuvt.shape),
            pl.BlockSpec((1, TS, LANES), lambda b, s: (b, s, 0)),
            pl.BlockSpec((1, TS, LANES), lambda b, s: (b, s, 0)),
        ],
        out_specs=out_specs,
        compiler_params=_params(2),
        name="mla_pre",
    )(x, mod, g, wdq, gq, wuq, wuqr, wdkv, wdkvr, gkv, wuk, wuvt, cos, sin)


def _attn_kernel(q_ref, k_ref, vt_ref, o_ref):
    seq = q_ref.shape[1]
    key = lax.broadcasted_iota(jnp.int32, (TQ, TQ), 0)
    qry = lax.broadcasted_iota(jnp.int32, (TQ, TQ), 1)
    causal = key <= qry
    blocks = [(h, qi) for h in range(ATTN_HEADS) for qi in reversed(range(seq // TQ))]

    def logits(h, qi):
        w = TQ * (qi + 1)
        q = q_ref[0, w - TQ:w, LANES * h:LANES * (h + 1)]
        return _dot_nt(k_ref[0, 0:w, LANES * h:LANES * (h + 1)], q)

    def finish(h, qi, s):
        q0 = TQ * qi
        w = q0 + TQ
        s_diag = jnp.where(causal, s[q0:w], -jnp.inf)
        m = jnp.max(s_diag, axis=0, keepdims=True)
        if qi > 0:
            m = jnp.maximum(m, jnp.max(s[0:q0], axis=0, keepdims=True))
        rows = [s[c:c + MXU_TILE] for c in range(0, q0, MXU_TILE)]
        rows += [s_diag[c:c + MXU_TILE] for c in range(0, TQ, MXU_TILE)]
        p = jnp.concatenate([jnp.exp2((r - m).astype(BF16)) for r in rows], axis=0)
        acc = _dot(vt_ref[0, V_ROWS * h:V_ROWS * (h + 1), 0:w], p)
        o_ref[0, HEAD_DIM * h:HEAD_DIM * (h + 1), q0:w] = (
            acc[:HEAD_DIM] / acc[HEAD_DIM:HEAD_DIM + 1]).astype(BF16)

    ahead = [logits(*blk) for blk in blocks[:LOGITS_AHEAD]]
    for i, blk in enumerate(blocks):
        if i + LOGITS_AHEAD < len(blocks):
            ahead.append(logits(*blocks[i + LOGITS_AHEAD]))
        finish(*blk, ahead.pop(0))


def _attn_call(qp, kp, vt):
    bsz, seq, _ = qp.shape
    return pl.pallas_call(
        _attn_kernel,
        out_shape=jax.ShapeDtypeStruct((bsz, D_MODEL, seq), BF16),
        grid=(bsz, HEADS // ATTN_HEADS),
        in_specs=[
            pl.BlockSpec((1, seq, ATTN_HEADS * LANES), lambda b, h: (b, 0, h)),
            pl.BlockSpec((1, seq, ATTN_HEADS * LANES), lambda b, h: (b, 0, h)),
            pl.BlockSpec((1, ATTN_HEADS * V_ROWS, seq), lambda b, h: (b, h, 0)),
        ],
        out_specs=pl.BlockSpec((1, ATTN_HEADS * HEAD_DIM, seq), lambda b, h: (b, h, 0)),
        compiler_params=_params(2),
        name="causal_attn",
    )(qp, kp, vt)


FF_CHUNK = 1024


def _post_kernel(ot_ref, x_ref, mod_ref, g_ref, wo_ref, w1_ref, w2_ref, gf_ref, out_ref,
                 *, final):
    g_m = mod_ref[0, 0, 2:3, :]
    sh_f = mod_ref[0, 0, 3:4, :]
    sc_f = mod_ref[0, 0, 4:5, :]
    g_f = mod_ref[0, 0, 5:6, :]
    x1 = x_ref[0] + g_m * _dot_tn(ot_ref[0], wo_ref[...])
    hb = (_rms(x1, g_ref[...]) * (1.0 + sc_f) + sh_f).astype(BF16)
    acc = jnp.zeros_like(x1)
    for j in range(D_FF // FF_CHUNK):
        a = jnp.maximum(_dot(hb, w1_ref[:, FF_CHUNK * j:FF_CHUNK * (j + 1)]), 0.0)
        acc = acc + _dot((a * a).astype(BF16), w2_ref[FF_CHUNK * j:FF_CHUNK * (j + 1), :])
    x2 = x1 + g_f * acc
    if final:
        x2 = _rms(x2, gf_ref[...])
    out_ref[0] = x2


def _post_call(ot, x, mod, layer, g, wo, w1, w2, gf, final):
    bsz, seq, d = x.shape
    return pl.pallas_call(
        functools.partial(_post_kernel, final=final),
        out_shape=jax.ShapeDtypeStruct((bsz, seq, d), F32),
        grid=(bsz, seq // TS),
        in_specs=[
            pl.BlockSpec((1, d, TS), lambda b, s: (b, 0, s)),
            pl.BlockSpec((1, TS, d), lambda b, s: (b, s, 0)),
            pl.BlockSpec((1, 1, N_MOD, d), lambda b, s: (layer, b, 0, 0)),
            _const_spec(g.shape),
            _const_spec(wo.shape), _const_spec(w1.shape), _const_spec(w2.shape),
            _const_spec(gf.shape),
        ],
        out_specs=pl.BlockSpec((1, TS, d), lambda b, s: (b, s, 0)),
        compiler_params=_params(2),
        name="post_mlp",
    )(ot, x, mod, g, wo, w1, w2, gf)


def _pad_heads(w, used):
    k = w.shape[0]
    return jnp.pad(w.reshape(k, HEADS, used),
                   ((0, 0), (0, 0), (0, LANES - used))).reshape(k, HP)


def kernel(x, c, positions, ada_w, ada_b, norm_mix_g, norm_mlp_g, fox_w_in, fox_b_f, fox_w_out, mla_w_dq, mla_q_norm_g, mla_w_uq, mla_w_dkv, mla_kv_norm_g, mla_w_ukv, mla_w_out, mlp_w1, mlp_w2, final_norm_g):
    bsz, seq, d = x.shape
    assert d == D_MODEL and seq % TS == 0 and seq % TQ == 0
    assert ada_w.shape == (DEPTH, d, N_MOD * d)
    assert fox_w_in.shape[1:] == (d, 3 * d + HEADS)
    assert mla_w_uq.shape[1:] == (Q_RANK, HEADS * (HEAD_DIM + ROPE_DIM))
    assert mla_w_dkv.shape[1:] == (d, KV_RANK + ROPE_DIM)
    assert mla_w_ukv.shape[1:] == (KV_RANK, HEADS * 2 * HEAD_DIM)

    mod = _mod_call(c, ada_w, ada_b).reshape(DEPTH, bsz, N_MOD, d)
    tables = _rope_call(positions)
    gf = final_norm_g.reshape(1, d)

    for i in range(DEPTH):
        j = i // 2
        g_mix = norm_mix_g[i].reshape(1, d)
        if i % 2 == 0:
            w_in = fox_w_in[j]
            wqf = jnp.pad(jnp.concatenate([w_in[:, :d], w_in[:, 3 * d:]], axis=1),
                          ((0, 0), (0, LANES - HEADS))).astype(BF16)
            wk = w_in[:, d:2 * d].astype(BF16)
            wvt = w_in[:, 2 * d:3 * d].T.astype(BF16)
            bf = fox_b_f[j].reshape(1, HEADS)
            qp, kp, vt = _fox_pre_call(x, mod, i, g_mix, wqf, wk, wvt, bf)
            wo = fox_w_out[j].astype(BF16)
        else:
            wdq = mla_w_dq[j].astype(BF16)
            gq = mla_q_norm_g[j].reshape(1, Q_RANK)
            w_uq = mla_w_uq[j].reshape(Q_RANK, HEADS, HEAD_DIM + ROPE_DIM)
            wuq = _pad_heads(mla_w_uq[j], HEAD_DIM + ROPE_DIM).astype(BF16)
            wuqr = jnp.pad(
                jnp.concatenate([-w_uq[:, :, HEAD_DIM + ROPE_HALF:],
                                 w_uq[:, :, HEAD_DIM:HEAD_DIM + ROPE_HALF]], axis=2),
                ((0, 0), (0, 0), (HEAD_DIM, LANES - HEAD_DIM - ROPE_DIM)),
            ).reshape(Q_RANK, HP).astype(BF16)
            wdkv = mla_w_dkv[j][:, :KV_RANK].astype(BF16)
            wdkvr = jnp.pad(mla_w_dkv[j][:, KV_RANK:],
                            ((0, 0), (HEAD_DIM, LANES - HEAD_DIM - ROPE_DIM))).astype(BF16)
            gkv = mla_kv_norm_g[j].reshape(1, KV_RANK)
            w_ukv = mla_w_ukv[j].reshape(KV_RANK, HEADS, 2 * HEAD_DIM)
            wuk = _pad_heads(w_ukv[:, :, :HEAD_DIM].reshape(KV_RANK, HEADS * HEAD_DIM),
                             HEAD_DIM).astype(BF16)
            wuvt = w_ukv[:, :, HEAD_DIM:].reshape(KV_RANK, HEADS * HEAD_DIM).T.astype(BF16)
            qp, kp, vt = _mla_pre_call(x, mod, i, g_mix, wdq, gq, wuq, wuqr, wdkv, wdkvr, gkv,
                                       wuk, wuvt, tables)
            wo = mla_w_out[j].astype(BF16)
        ot = _attn_call(qp, kp, vt)
        x = _post_call(ot, x, mod, i, norm_mlp_g[i].reshape(1, d), wo,
                       mlp_w1[i].astype(BF16), mlp_w2[i].astype(BF16), gf,
                       final=(i == DEPTH - 1))
    return x
```

```python
import functools

import jax
import jax.numpy as jnp
from jax import lax
from jax.experimental import pallas as pl
from jax.experimental.pallas import tpu as pltpu

D_MODEL = 1024
DEPTH = 4
HEADS = 16
HEAD_DIM = 64
ROPE_DIM = 32
ROPE_HALF = ROPE_DIM // 2
Q_RANK = 256
KV_RANK = 128
D_FF = 4 * D_MODEL
ROPE_THETA = 10000.0
NORM_EPS = 1e-6
N_MOD = 6

LANES = 128
MXU_TILE = 256
HP = HEADS * LANES
N_SPLIT = 3

V_ROWS = HEAD_DIM + 16
LOG2E = 1.4426950408889634

TS = 512
TQ = 512
ATTN_HEADS = 4
LOGITS_AHEAD = 1
VMEM_LIMIT = 56 * 1024 * 1024

BF16 = jnp.bfloat16
F32 = jnp.float32
NT_DIMS = (((1,), (1,)), ((), ()))
TN_DIMS = (((0,), (0,)), ((), ()))


def _dot(a, b):
    return jnp.dot(a, b, preferred_element_type=F32)


def _dot_nt(a, b):
    return lax.dot_general(a, b, NT_DIMS, preferred_element_type=F32)


def _dot_tn(a, b):
    return lax.dot_general(a, b, TN_DIMS, preferred_element_type=F32)


def _dot_exact(a, b):
    return jnp.dot(a, b, preferred_element_type=F32, precision=lax.Precision.HIGHEST)


def _rms(x, g):
    ms = jnp.mean(x * x, axis=-1, keepdims=True)
    return x * lax.rsqrt(ms + NORM_EPS) * g


def _const_spec(shape):
    zeros = (0,) * len(shape)
    return pl.BlockSpec(shape, lambda *_: zeros, pipeline_mode=pl.Buffered(1))


def _params(n_axes):
    return pltpu.CompilerParams(
        dimension_semantics=("arbitrary",) * n_axes,
        vmem_limit_bytes=VMEM_LIMIT)


MOD_TN = 1024


def _mod_kernel(c_ref, w_ref, b_ref, o_ref):
    ca = jax.nn.silu(c_ref[...])
    o_ref[0] = _dot_exact(ca, w_ref[0]) + b_ref[0]


def _mod_call(c, ada_w, ada_b):
    depth, d, n = ada_w.shape
    bsz = c.shape[0]
    return pl.pallas_call(
        _mod_kernel,
        out_shape=jax.ShapeDtypeStruct((depth, bsz, n), F32),
        grid=(depth, n // MOD_TN),
        in_specs=[
            pl.BlockSpec((bsz, d), lambda l, j: (0, 0)),
            pl.BlockSpec((1, d, MOD_TN), lambda l, j: (l, 0, j)),
            pl.BlockSpec((1, 1, MOD_TN), lambda l, j: (l, 0, j)),
        ],
        out_specs=pl.BlockSpec((1, bsz, MOD_TN), lambda l, j: (l, 0, j)),
        compiler_params=_params(2),
        name="adaln_mod",
    )(c, ada_w, ada_b.reshape(depth, 1, n))


def _rope_kernel(pos_ref, c_ref, s_ref):
    seq = pos_ref.shape[1]
    lane = lax.broadcasted_iota(jnp.int32, (1, LANES), 1)
    is_rope = (lane >= HEAD_DIM) & (lane < HEAD_DIM + ROPE_DIM)
    fidx = ((lane - HEAD_DIM) % ROPE_HALF).astype(F32)
    inv_freq = ROPE_THETA ** (-(2.0 * fidx) / ROPE_DIM)
    ang = pos_ref[0].astype(F32) * inv_freq
    rope_mask = jnp.broadcast_to(is_rope, (seq, LANES))
    pass_mask = jnp.broadcast_to(lane < HEAD_DIM, (seq, LANES))
    c_ref[0] = jnp.where(rope_mask, jnp.cos(ang), jnp.where(pass_mask, 1.0, 0.0))
    s_ref[0] = jnp.where(rope_mask, jnp.sin(ang), 0.0)


def _rope_call(positions):
    bsz, seq = positions.shape
    return pl.pallas_call(
        _rope_kernel,
        out_shape=(
            jax.ShapeDtypeStruct((bsz, seq, LANES), F32),
            jax.ShapeDtypeStruct((bsz, seq, LANES), F32),
        ),
        grid=(bsz,),
        in_specs=[pl.BlockSpec((1, seq, 1), lambda b: (b, 0, 0))],
        out_specs=(
            pl.BlockSpec((1, seq, LANES), lambda b: (b, 0, 0)),
            pl.BlockSpec((1, seq, LANES), lambda b: (b, 0, 0)),
        ),
        compiler_params=_params(1),
        name="rope_tables",
    )(positions.reshape(bsz, seq, 1))


def _modulated_norm(x_ref, mod_ref, g_ref, shift_row):
    x = x_ref[0]
    sh = mod_ref[0, 0, shift_row:shift_row + 1, :]
    sc = mod_ref[0, 0, shift_row + 1:shift_row + 2, :]
    return _rms(x, g_ref[...]) * (1.0 + sc) + sh


def _split3(x):
    hi = x.astype(BF16)
    r1 = x - hi.astype(F32)
    mid = r1.astype(BF16)
    lo = (r1 - mid.astype(F32)).astype(BF16)
    return hi, mid, lo


def _log_sigmoid(x):
    return jnp.minimum(x, 0.0) - jnp.log1p(jnp.exp(-jnp.abs(x)))


def _rope_rotate(blk, cos, sin, first_half):
    rot = jnp.where(first_half,
                    -pltpu.roll(blk, LANES - ROPE_HALF, axis=1),
                    pltpu.roll(blk, ROPE_HALF, axis=1))
    return blk * cos + rot * sin


def _store_vt(vt_ref, v_t):
    ts = v_t.shape[1]
    sub = lax.broadcasted_iota(jnp.int32, (V_ROWS - HEAD_DIM, ts), 0)
    ones_rows = jnp.where(sub == 0, 1.0, 0.0).astype(BF16)
    for hd in range(HEADS):
        r0 = V_ROWS * hd
        vt_ref[0, r0:r0 + HEAD_DIM, :] = v_t[HEAD_DIM * hd:HEAD_DIM * (hd + 1), :].astype(BF16)
        vt_ref[0, r0 + HEAD_DIM:r0 + V_ROWS, :] = ones_rows


def _fox_pre_kernel(x_ref, mod_ref, g_ref, wqf_ref, wk_ref, wvt_ref, bf_ref,
                    qp_ref, kp_ref, vt_ref, carry_ref):
    ts = x_ref.shape[1]

    @pl.when(pl.program_id(1) == 0)
    def _():
        carry_ref[...] = jnp.zeros_like(carry_ref)

    hb = _modulated_norm(x_ref, mod_ref, g_ref, 0).astype(BF16)
    d = hb.shape[1]
    qf = _dot(hb, wqf_ref[...])
    k = _dot(hb, wk_ref[...])
    q = qf[:, :d] * (HEAD_DIM ** -0.5 * LOG2E)

    lf = _log_sigmoid(qf[:, d:d + HEADS] + bf_ref[...])
    r = lax.broadcasted_iota(jnp.int32, (ts, ts), 0)
    c = lax.broadcasted_iota(jnp.int32, (ts, ts), 1)
    lower = jnp.where(c <= r, 1.0, 0.0).astype(BF16)
    sums = _dot(lower, jnp.concatenate(_split3(lf), axis=1))
    f_cum = carry_ref[...]
    for j in range(N_SPLIT):
        f_cum = f_cum + sums[:, HEADS * j:HEADS * (j + 1)]
    carry_ref[...] = f_cum[ts - 1:ts, :]

    row = lax.broadcasted_iota(jnp.int32, (N_SPLIT * HEADS, LANES), 0)
    lsel = (lax.broadcasted_iota(jnp.int32, (N_SPLIT * HEADS, LANES), 1)
            - 2 * N_SPLIT * (row % HEADS) - row // HEADS)
    sel = jnp.where(lsel == 0, 1.0, jnp.where(lsel == N_SPLIT, -1.0, 0.0)).astype(BF16)
    g = _dot(jnp.concatenate(_split3(f_cum * LOG2E), axis=1), sel)
    v_t = _dot_nt(wvt_ref[...], hb)

    lane = lax.broadcasted_iota(jnp.int32, (ts, LANES), 1)
    in_head = lane < HEAD_DIM
    in_pos = lane < HEAD_DIM + N_SPLIT
    in_ext = lane < HEAD_DIM + 2 * N_SPLIT

    def store_heads(out_ref, proj, is_q):
        for hd in range(HEADS):
            blk = proj[:, LANES * (hd // 2):LANES * (hd // 2 + 1)]
            if hd % 2 == 1:
                blk = pltpu.roll(blk, HEAD_DIM, axis=1)
            rot = pltpu.roll(g, (HEAD_DIM - 2 * N_SPLIT * hd) % LANES, axis=1)
            if is_q:
                ext = jnp.where(in_pos, rot, jnp.where(in_ext, 1.0, 0.0))
            else:
                ext = jnp.where(in_pos, 1.0, jnp.where(in_ext, rot, 0.0))
            out_ref[0, :, LANES * hd:LANES * (hd + 1)] = (
                jnp.where(in_head, blk, ext).astype(BF16))

    store_heads(qp_ref, q, True)
    store_heads(kp_ref, k, False)
    _store_vt(vt_ref, v_t)


def _pre_out(bsz, seq):
    out_shape = (
        jax.ShapeDtypeStruct((bsz, seq, HP), BF16),
        jax.ShapeDtypeStruct((bsz, seq, HP), BF16),
        jax.ShapeDtypeStruct((bsz, HEADS * V_ROWS, seq), BF16),
    )
    out_specs = (
        pl.BlockSpec((1, TS, HP), lambda b, s: (b, s, 0)),
        pl.BlockSpec((1, TS, HP), lambda b, s: (b, s, 0)),
        pl.BlockSpec((1, HEADS * V_ROWS, TS), lambda b, s: (b, 0, s)),
    )
    return out_shape, out_specs


def _fox_pre_call(x, mod, layer, g, wqf, wk, wvt, bf):
    bsz, seq, d = x.shape
    out_shape, out_specs = _pre_out(bsz, seq)
    return pl.pallas_call(
        _fox_pre_kernel,
        out_shape=out_shape,
        grid=(bsz, seq // TS),
        in_specs=[
            pl.BlockSpec((1, TS, d), lambda b, s: (b, s, 0)),
            pl.BlockSpec((1, 1, N_MOD, d), lambda b, s: (layer, b, 0, 0)),
            _const_spec(g.shape),
            _const_spec(wqf.shape), _const_spec(wk.shape), _const_spec(wvt.shape),
            _const_spec(bf.shape),
        ],
        out_specs=out_specs,
        scratch_shapes=[pltpu.VMEM((1, HEADS), F32)],
        compiler_params=_params(2),
        name="fox_pre",
    )(x, mod, g, wqf, wk, wvt, bf)


def _mla_pre_kernel(x_ref, mod_ref, g_ref, wdq_ref, gq_ref, wuq_ref, wuqr_ref, wdkv_ref, wdkvr_ref,
                    gkv_ref, wuk_ref, wuvt_ref, cos_ref, sin_ref, qp_ref, kp_ref, vt_ref):
    ts = x_ref.shape[1]
    hb = _modulated_norm(x_ref, mod_ref, g_ref, 0).astype(BF16)
    cos = cos_ref[0]
    sin = sin_ref[0]
    lane = lax.broadcasted_iota(jnp.int32, (ts, LANES), 1)
    first_half = lane < HEAD_DIM + ROPE_HALF
    scale = (HEAD_DIM + ROPE_DIM) ** -0.5 * LOG2E

    cq = _rms(_dot(hb, wdq_ref[...]), gq_ref[...]).astype(BF16)
    q = _dot(cq, wuq_ref[...])
    q_rot = _dot(cq, wuqr_ref[...])
    ckv = _rms(_dot(hb, wdkv_ref[...]), gkv_ref[...]).astype(BF16)
    _store_vt(vt_ref, _dot_nt(wuvt_ref[...], ckv))
    k_nope = _dot(ckv, wuk_ref[...])
    k_rope = _rope_rotate(_dot(hb, wdkvr_ref[...]), cos, sin, first_half)
    for hd in range(HEADS):
        out = slice(LANES * hd, LANES * (hd + 1))
        qp_ref[0, :, out] = ((q[:, out] * cos + q_rot[:, out] * sin) * scale).astype(BF16)
        kp_ref[0, :, out] = (k_nope[:, out] + k_rope).astype(BF16)


def _mla_pre_call(x, mod, layer, g, wdq, gq, wuq, wuqr, wdkv, wdkvr, gkv, wuk, wuvt, tables):
    bsz, seq, d = x.shape
    cos, sin = tables
    out_shape, out_specs = _pre_out(bsz, seq)
    return pl.pallas_call(
        _mla_pre_kernel,
        out_shape=out_shape,
        grid=(bsz, seq // TS),
        in_specs=[
            pl.BlockSpec((1, TS, d), lambda b, s: (b, s, 0)),
            pl.BlockSpec((1, 1, N_MOD, d), lambda b, s: (layer, b, 0, 0)),
            _const_spec(g.shape),
            _const_spec(wdq.shape), _const_spec(gq.shape), _const_spec(wuq.shape),
            _const_spec(wuqr.shape),
            _const_spec(wdkv.shape), _const_spec(wdkvr.shape), _const_spec(gkv.shape),
            _const_spec(wuk.shape), _const_spec(wuvt.shape),
            pl.BlockSpec((1, TS, LANES), lambda b, s: (b, s, 0)),
            pl.BlockSpec((1, TS, LANES), lambda b, s: (b, s, 0)),
        ],
        out_specs=out_specs,
        compiler_params=_params(2),
        name="mla_pre",
    )(x, mod, g, wdq, gq, wuq, wuqr, wdkv, wdkvr, gkv, wuk, wuvt, cos, sin)


def _attn_kernel(q_ref, k_ref, vt_ref, o_ref):
    seq = q_ref.shape[1]
    key = lax.broadcasted_iota(jnp.int32, (MXU_TILE, MXU_TILE), 0)
    qry = lax.broadcasted_iota(jnp.int32, (MXU_TILE, MXU_TILE), 1)
    tri = key <= qry
    n_sub = TQ // MXU_TILE
    order = list(range(seq // TQ))
    blocks = [(h, qi) for h in range(ATTN_HEADS) for qi in (order if h % 2 == 0 else order[::-1])]

    def logits(h, qi, c):
        q = q_ref[0, TQ * qi:TQ * (qi + 1), LANES * h:LANES * (h + 1)]
        return _dot_nt(k_ref[0, TQ * c:TQ * (c + 1), LANES * h:LANES * (h + 1)], q)

    def softmax_weights(s_chunks):
        diag = s_chunks[-1]
        tile = lambda i, j: diag[MXU_TILE * i:MXU_TILE * (i + 1), MXU_TILE * j:MXU_TILE * (j + 1)]
        d_tiles = [[tile(i, j) if i < j else jnp.where(tri, tile(i, j), -jnp.inf)
                    for i in range(j + 1)] for j in range(n_sub)]
        m = jnp.concatenate(
            [functools.reduce(jnp.maximum, [jnp.max(t, axis=0, keepdims=True) for t in col])
             for col in d_tiles], axis=1)
        for sc in s_chunks[:-1]:
            m = jnp.maximum(m, jnp.max(sc, axis=0, keepdims=True))
        p = [jnp.exp2((sc - m).astype(BF16)) for sc in s_chunks[:-1]]
        cols = []
        for j, col in enumerate(d_tiles):
            mj = m[:, MXU_TILE * j:MXU_TILE * (j + 1)]
            pt = [jnp.exp2((t - mj).astype(BF16)) for t in col]
            if j + 1 < n_sub:
                pt.append(jnp.zeros((MXU_TILE * (n_sub - 1 - j), MXU_TILE), BF16))
            cols.append(jnp.concatenate(pt, axis=0))
        return p + [jnp.concatenate(cols, axis=1)]

    def weighted_values(h, c, p):
        return _dot(vt_ref[0, V_ROWS * h:V_ROWS * (h + 1), TQ * c:TQ * (c + 1)], p)

    s_chunks = [logits(*blocks[0], c) for c in range(blocks[0][1] + 1)]
    for i, (h, qi) in enumerate(blocks):
        p_chunks = softmax_weights(s_chunks)
        nxt = blocks[i + 1] if i + 1 < len(blocks) else None
        n_next = nxt[1] + 1 if nxt else 0
        s_chunks, acc = [], 0.0
        for c in range(max(qi + 1, n_next)):
            if c < n_next:
                s_chunks.append(logits(*nxt, c))
            if c <= qi:
                acc = acc + weighted_values(h, c, p_chunks[c])
        o_ref[0, HEAD_DIM * h:HEAD_DIM * (h + 1), TQ * qi:TQ * (qi + 1)] = (
            acc[:HEAD_DIM] / acc[HEAD_DIM:HEAD_DIM + 1]).astype(BF16)


def _attn_call(qp, kp, vt):
    bsz, seq, _ = qp.shape
    return pl.pallas_call(
        _attn_kernel,
        out_shape=jax.ShapeDtypeStruct((bsz, D_MODEL, seq), BF16),
        grid=(bsz, HEADS // ATTN_HEADS),
        in_specs=[
            pl.BlockSpec((1, seq, ATTN_HEADS * LANES), lambda b, h: (b, 0, h)),
            pl.BlockSpec((1, seq, ATTN_HEADS * LANES), lambda b, h: (b, 0, h)),
            pl.BlockSpec((1, ATTN_HEADS * V_ROWS, seq), lambda b, h: (b, h, 0)),
        ],
        out_specs=pl.BlockSpec((1, ATTN_HEADS * HEAD_DIM, seq), lambda b, h: (b, h, 0)),
        compiler_params=_params(2),
        name="causal_attn",
    )(qp, kp, vt)


FF_CHUNK = 1024


def _post_kernel(ot_ref, x_ref, mod_ref, g_ref, wo_ref, w1_ref, w2_ref, gf_ref, out_ref,
                 *, final):
    g_m = mod_ref[0, 0, 2:3, :]
    sh_f = mod_ref[0, 0, 3:4, :]
    sc_f = mod_ref[0, 0, 4:5, :]
    g_f = mod_ref[0, 0, 5:6, :]
    x1 = x_ref[0] + g_m * _dot_tn(ot_ref[0], wo_ref[...])
    hb = (_rms(x1, g_ref[...]) * (1.0 + sc_f) + sh_f).astype(BF16)
    acc = jnp.zeros_like(x1)
    for j in range(D_FF // FF_CHUNK):
        a = jnp.maximum(_dot(hb, w1_ref[:, FF_CHUNK * j:FF_CHUNK * (j + 1)]), 0.0)
        acc = acc + _dot((a * a).astype(BF16), w2_ref[FF_CHUNK * j:FF_CHUNK * (j + 1), :])
    x2 = x1 + g_f * acc
    if final:
        x2 = _rms(x2, gf_ref[...])
    out_ref[0] = x2


def _post_call(ot, x, mod, layer, g, wo, w1, w2, gf, final):
    bsz, seq, d = x.shape
    return pl.pallas_call(
        functools.partial(_post_kernel, final=final),
        out_shape=jax.ShapeDtypeStruct((bsz, seq, d), F32),
        grid=(bsz, seq // TS),
        in_specs=[
            pl.BlockSpec((1, d, TS), lambda b, s: (b, 0, s)),
            pl.BlockSpec((1, TS, d), lambda b, s: (b, s, 0)),
            pl.BlockSpec((1, 1, N_MOD, d), lambda b, s: (layer, b, 0, 0)),
            _const_spec(g.shape),
            _const_spec(wo.shape), _const_spec(w1.shape), _const_spec(w2.shape),
            _const_spec(gf.shape),
        ],
        out_specs=pl.BlockSpec((1, TS, d), lambda b, s: (b, s, 0)),
        compiler_params=_params(2),
        name="post_mlp",
    )(ot, x, mod, g, wo, w1, w2, gf)


def _pad_heads(w, used):
    k = w.shape[0]
    return jnp.pad(w.reshape(k, HEADS, used),
                   ((0, 0), (0, 0), (0, LANES - used))).reshape(k, HP)


def kernel(x, c, positions, ada_w, ada_b, norm_mix_g, norm_mlp_g, fox_w_in, fox_b_f, fox_w_out, mla_w_dq, mla_q_norm_g, mla_w_uq, mla_w_dkv, mla_kv_norm_g, mla_w_ukv, mla_w_out, mlp_w1, mlp_w2, final_norm_g):
    bsz, seq, d = x.shape
    assert d == D_MODEL and seq % TS == 0 and seq % TQ == 0
    assert ada_w.shape == (DEPTH, d, N_MOD * d)
    assert fox_w_in.shape[1:] == (d, 3 * d + HEADS)
    assert mla_w_uq.shape[1:] == (Q_RANK, HEADS * (HEAD_DIM + ROPE_DIM))
    assert mla_w_dkv.shape[1:] == (d, KV_RANK + ROPE_DIM)
    assert mla_w_ukv.shape[1:] == (KV_RANK, HEADS * 2 * HEAD_DIM)

    mod = _mod_call(c, ada_w, ada_b).reshape(DEPTH, bsz, N_MOD, d)
    tables = _rope_call(positions)
    gf = final_norm_g.reshape(1, d)

    for i in range(DEPTH):
        j = i // 2
        g_mix = norm_mix_g[i].reshape(1, d)
        if i % 2 == 0:
            w_in = fox_w_in[j]
            wqf = jnp.pad(jnp.concatenate([w_in[:, :d], w_in[:, 3 * d:]], axis=1),
                          ((0, 0), (0, LANES - HEADS))).astype(BF16)
            wk = w_in[:, d:2 * d].astype(BF16)
            wvt = w_in[:, 2 * d:3 * d].T.astype(BF16)
            bf = fox_b_f[j].reshape(1, HEADS)
            qp, kp, vt = _fox_pre_call(x, mod, i, g_mix, wqf, wk, wvt, bf)
            wo = fox_w_out[j].astype(BF16)
        else:
            wdq = mla_w_dq[j].astype(BF16)
            gq = mla_q_norm_g[j].reshape(1, Q_RANK)
            w_uq = mla_w_uq[j].reshape(Q_RANK, HEADS, HEAD_DIM + ROPE_DIM)
            wuq = _pad_heads(mla_w_uq[j], HEAD_DIM + ROPE_DIM).astype(BF16)
            wuqr = jnp.pad(
                jnp.concatenate([-w_uq[:, :, HEAD_DIM + ROPE_HALF:],
                                 w_uq[:, :, HEAD_DIM:HEAD_DIM + ROPE_HALF]], axis=2),
                ((0, 0), (0, 0), (HEAD_DIM, LANES - HEAD_DIM - ROPE_DIM)),
            ).reshape(Q_RANK, HP).astype(BF16)
            wdkv = mla_w_dkv[j][:, :KV_RANK].astype(BF16)
            wdkvr = jnp.pad(mla_w_dkv[j][:, KV_RANK:],
                            ((0, 0), (HEAD_DIM, LANES - HEAD_DIM - ROPE_DIM))).astype(BF16)
            gkv = mla_kv_norm_g[j].reshape(1, KV_RANK)
            w_ukv = mla_w_ukv[j].reshape(KV_RANK, HEADS, 2 * HEAD_DIM)
            wuk = _pad_heads(w_ukv[:, :, :HEAD_DIM].reshape(KV_RANK, HEADS * HEAD_DIM),
                             HEAD_DIM).astype(BF16)
            wuvt = w_ukv[:, :, HEAD_DIM:].reshape(KV_RANK, HEADS * HEAD_DIM).T.astype(BF16)
            qp, kp, vt = _mla_pre_call(x, mod, i, g_mix, wdq, gq, wuq, wuqr, wdkv, wdkvr, gkv,
                                       wuk, wuvt, tables)
            wo = mla_w_out[j].astype(BF16)
        ot = _attn_call(qp, kp, vt)
        x = _post_call(ot, x, mod, i, norm_mlp_g[i].reshape(1, d), wo,
                       mlp_w1[i].astype(BF16), mlp_w2[i].astype(BF16), gf,
                       final=(i == DEPTH - 1))
    return x
```

```python
import functools

import jax
import jax.numpy as jnp
from jax import lax
from jax.experimental import pallas as pl
from jax.experimental.pallas import tpu as pltpu

D_MODEL = 1024
DEPTH = 4
HEADS = 16
HEAD_DIM = 64
ROPE_DIM = 32
ROPE_HALF = ROPE_DIM // 2
Q_RANK = 256
KV_RANK = 128
D_FF = 4 * D_MODEL
ROPE_THETA = 10000.0
NORM_EPS = 1e-6
N_MOD = 6

LANES = 128
MXU_TILE = 256
HP = HEADS * LANES
N_SPLIT = 3

V_ROWS = HEAD_DIM + 16
LOG2E = 1.4426950408889634

TS = 512
TQ = 512
ATTN_HEADS = 4
LOGITS_CHUNKS = 2
VMEM_LIMIT = 56 * 1024 * 1024

BF16 = jnp.bfloat16
F32 = jnp.float32
NT_DIMS = (((1,), (1,)), ((), ()))
TN_DIMS = (((0,), (0,)), ((), ()))


def _dot(a, b):
    return jnp.dot(a, b, preferred_element_type=F32)


def _dot_nt(a, b):
    return lax.dot_general(a, b, NT_DIMS, preferred_element_type=F32)


def _dot_tn(a, b):
    return lax.dot_general(a, b, TN_DIMS, preferred_element_type=F32)


def _dot_exact(a, b):
    return jnp.dot(a, b, preferred_element_type=F32, precision=lax.Precision.HIGHEST)


def _rms(x, g):
    ms = jnp.mean(x * x, axis=-1, keepdims=True)
    return x * lax.rsqrt(ms + NORM_EPS) * g


def _const_spec(shape):
    zeros = (0,) * len(shape)
    return pl.BlockSpec(shape, lambda *_: zeros, pipeline_mode=pl.Buffered(1))


def _params(n_axes):
    return pltpu.CompilerParams(
        dimension_semantics=("arbitrary",) * n_axes,
        vmem_limit_bytes=VMEM_LIMIT)


MOD_TN = 1024


def _mod_kernel(c_ref, w_ref, b_ref, o_ref):
    ca = jax.nn.silu(c_ref[...])
    o_ref[0] = _dot_exact(ca, w_ref[0]) + b_ref[0]


def _mod_call(c, ada_w, ada_b):
    depth, d, n = ada_w.shape
    bsz = c.shape[0]
    return pl.pallas_call(
        _mod_kernel,
        out_shape=jax.ShapeDtypeStruct((depth, bsz, n), F32),
        grid=(depth, n // MOD_TN),
        in_specs=[
            pl.BlockSpec((bsz, d), lambda l, j: (0, 0)),
            pl.BlockSpec((1, d, MOD_TN), lambda l, j: (l, 0, j)),
            pl.BlockSpec((1, 1, MOD_TN), lambda l, j: (l, 0, j)),
        ],
        out_specs=pl.BlockSpec((1, bsz, MOD_TN), lambda l, j: (l, 0, j)),
        compiler_params=_params(2),
        name="adaln_mod",
    )(c, ada_w, ada_b.reshape(depth, 1, n))


def _rope_kernel(pos_ref, c_ref, s_ref):
    seq = pos_ref.shape[1]
    lane = lax.broadcasted_iota(jnp.int32, (1, LANES), 1)
    is_rope = (lane >= HEAD_DIM) & (lane < HEAD_DIM + ROPE_DIM)
    fidx = ((lane - HEAD_DIM) % ROPE_HALF).astype(F32)
    inv_freq = ROPE_THETA ** (-(2.0 * fidx) / ROPE_DIM)
    ang = pos_ref[0].astype(F32) * inv_freq
    rope_mask = jnp.broadcast_to(is_rope, (seq, LANES))
    pass_mask = jnp.broadcast_to(lane < HEAD_DIM, (seq, LANES))
    c_ref[0] = jnp.where(rope_mask, jnp.cos(ang), jnp.where(pass_mask, 1.0, 0.0))
    s_ref[0] = jnp.where(rope_mask, jnp.sin(ang), 0.0)


def _rope_call(positions):
    bsz, seq = positions.shape
    return pl.pallas_call(
        _rope_kernel,
        out_shape=(
            jax.ShapeDtypeStruct((bsz, seq, LANES), F32),
            jax.ShapeDtypeStruct((bsz, seq, LANES), F32),
        ),
        grid=(bsz,),
        in_specs=[pl.BlockSpec((1, seq, 1), lambda b: (b, 0, 0))],
        out_specs=(
            pl.BlockSpec((1, seq, LANES), lambda b: (b, 0, 0)),
            pl.BlockSpec((1, seq, LANES), lambda b: (b, 0, 0)),
        ),
        compiler_params=_params(1),
        name="rope_tables",
    )(positions.reshape(bsz, seq, 1))


def _modulated_norm(x_ref, mod_ref, g_ref, shift_row):
    x = x_ref[0]
    sh = mod_ref[0, 0, shift_row:shift_row + 1, :]
    sc = mod_ref[0, 0, shift_row + 1:shift_row + 2, :]
    return _rms(x, g_ref[...]) * (1.0 + sc) + sh


def _split3(x):
    hi = x.astype(BF16)
    r1 = x - hi.astype(F32)
    mid = r1.astype(BF16)
    lo = (r1 - mid.astype(F32)).astype(BF16)
    return hi, mid, lo


def _log_sigmoid(x):
    return jnp.minimum(x, 0.0) - jnp.log1p(jnp.exp(-jnp.abs(x)))


def _rope_rotate(blk, cos, sin, first_half):
    rot = jnp.where(first_half,
                    -pltpu.roll(blk, LANES - ROPE_HALF, axis=1),
                    pltpu.roll(blk, ROPE_HALF, axis=1))
    return blk * cos + rot * sin


def _store_vt(vt_ref, v_t):
    ts = v_t.shape[1]
    sub = lax.broadcasted_iota(jnp.int32, (V_ROWS - HEAD_DIM, ts), 0)
    ones_rows = jnp.where(sub == 0, 1.0, 0.0).astype(BF16)
    for hd in range(HEADS):
        r0 = V_ROWS * hd
        vt_ref[0, r0:r0 + HEAD_DIM, :] = v_t[HEAD_DIM * hd:HEAD_DIM * (hd + 1), :].astype(BF16)
        vt_ref[0, r0 + HEAD_DIM:r0 + V_ROWS, :] = ones_rows


def _fox_pre_kernel(x_ref, mod_ref, g_ref, wqf_ref, wk_ref, wvt_ref, bf_ref,
                    qp_ref, kp_ref, vt_ref, carry_ref):
    ts = x_ref.shape[1]

    @pl.when(pl.program_id(1) == 0)
    def _():
        carry_ref[...] = jnp.zeros_like(carry_ref)

    hb = _modulated_norm(x_ref, mod_ref, g_ref, 0).astype(BF16)
    d = hb.shape[1]
    qf = _dot(hb, wqf_ref[...])
    k = _dot(hb, wk_ref[...])
    q = qf[:, :d] * (HEAD_DIM ** -0.5 * LOG2E)

    lf = _log_sigmoid(qf[:, d:d + HEADS] + bf_ref[...])
    r = lax.broadcasted_iota(jnp.int32, (ts, ts), 0)
    c = lax.broadcasted_iota(jnp.int32, (ts, ts), 1)
    lower = jnp.where(c <= r, 1.0, 0.0).astype(BF16)
    sums = _dot(lower, jnp.concatenate(_split3(lf), axis=1))
    f_cum = carry_ref[...]
    for j in range(N_SPLIT):
        f_cum = f_cum + sums[:, HEADS * j:HEADS * (j + 1)]
    carry_ref[...] = f_cum[ts - 1:ts, :]

    row = lax.broadcasted_iota(jnp.int32, (N_SPLIT * HEADS, LANES), 0)
    lsel = (lax.broadcasted_iota(jnp.int32, (N_SPLIT * HEADS, LANES), 1)
            - 2 * N_SPLIT * (row % HEADS) - row // HEADS)
    sel = jnp.where(lsel == 0, 1.0, jnp.where(lsel == N_SPLIT, -1.0, 0.0)).astype(BF16)
    g = _dot(jnp.concatenate(_split3(f_cum * LOG2E), axis=1), sel)
    v_t = _dot_nt(wvt_ref[...], hb)

    lane = lax.broadcasted_iota(jnp.int32, (ts, LANES), 1)
    in_head = lane < HEAD_DIM
    in_pos = lane < HEAD_DIM + N_SPLIT
    in_ext = lane < HEAD_DIM + 2 * N_SPLIT

    def store_heads(out_ref, proj, is_q):
        for hd in range(HEADS):
            blk = proj[:, LANES * (hd // 2):LANES * (hd // 2 + 1)]
            if hd % 2 == 1:
                blk = pltpu.roll(blk, HEAD_DIM, axis=1)
            rot = pltpu.roll(g, (HEAD_DIM - 2 * N_SPLIT * hd) % LANES, axis=1)
            if is_q:
                ext = jnp.where(in_pos, rot, jnp.where(in_ext, 1.0, 0.0))
            else:
                ext = jnp.where(in_pos, 1.0, jnp.where(in_ext, rot, 0.0))
            out_ref[0, :, LANES * hd:LANES * (hd + 1)] = (
                jnp.where(in_head, blk, ext).astype(BF16))

    store_heads(qp_ref, q, True)
    store_heads(kp_ref, k, False)
    _store_vt(vt_ref, v_t)


def _pre_out(bsz, seq):
    out_shape = (
        jax.ShapeDtypeStruct((bsz, seq, HP), BF16),
        jax.ShapeDtypeStruct((bsz, seq, HP), BF16),
        jax.ShapeDtypeStruct((bsz, HEADS * V_ROWS, seq), BF16),
    )
    out_specs = (
        pl.BlockSpec((1, TS, HP), lambda b, s: (b, s, 0)),
        pl.BlockSpec((1, TS, HP), lambda b, s: (b, s, 0)),
        pl.BlockSpec((1, HEADS * V_ROWS, TS), lambda b, s: (b, 0, s)),
    )
    return out_shape, out_specs


def _fox_pre_call(x, mod, layer, g, wqf, wk, wvt, bf):
    bsz, seq, d = x.shape
    out_shape, out_specs = _pre_out(bsz, seq)
    return pl.pallas_call(
        _fox_pre_kernel,
        out_shape=out_shape,
        grid=(bsz, seq // TS),
        in_specs=[
            pl.BlockSpec((1, TS, d), lambda b, s: (b, s, 0)),
            pl.BlockSpec((1, 1, N_MOD, d), lambda b, s: (layer, b, 0, 0)),
            _const_spec(g.shape),
            _const_spec(wqf.shape), _const_spec(wk.shape), _const_spec(wvt.shape),
            _const_spec(bf.shape),
        ],
        out_specs=out_specs,
        scratch_shapes=[pltpu.VMEM((1, HEADS), F32)],
        compiler_params=_params(2),
        name="fox_pre",
    )(x, mod, g, wqf, wk, wvt, bf)


def _mla_pre_kernel(x_ref, mod_ref, g_ref, wdq_ref, gq_ref, wuq_ref, wuqr_ref, wdkv_ref, wdkvr_ref,
                    gkv_ref, wuk_ref, wuvt_ref, cos_ref, sin_ref, qp_ref, kp_ref, vt_ref):
    ts = x_ref.shape[1]
    hb = _modulated_norm(x_ref, mod_ref, g_ref, 0).astype(BF16)
    cos = cos_ref[0]
    sin = sin_ref[0]
    lane = lax.broadcasted_iota(jnp.int32, (ts, LANES), 1)
    first_half = lane < HEAD_DIM + ROPE_HALF
    scale = (HEAD_DIM + ROPE_DIM) ** -0.5 * LOG2E

    cq = _rms(_dot(hb, wdq_ref[...]), gq_ref[...]).astype(BF16)
    ckv = _rms(_dot(hb, wdkv_ref[...]), gkv_ref[...]).astype(BF16)
    k_rope = _rope_rotate(_dot(hb, wdkvr_ref[...]), cos, sin, first_half)
    q = _dot(cq, wuq_ref[...])
    q_rot = _dot(cq, wuqr_ref[...])
    k_nope = _dot(ckv, wuk_ref[...])
    for hd in range(HEADS):
        out = slice(LANES * hd, LANES * (hd + 1))
        qp_ref[0, :, out] = ((q[:, out] * cos + q_rot[:, out] * sin) * scale).astype(BF16)
    v_t = _dot_nt(wuvt_ref[...], ckv)
    for hd in range(HEADS):
        out = slice(LANES * hd, LANES * (hd + 1))
        kp_ref[0, :, out] = (k_nope[:, out] + k_rope).astype(BF16)
    _store_vt(vt_ref, v_t)


def _mla_pre_call(x, mod, layer, g, wdq, gq, wuq, wuqr, wdkv, wdkvr, gkv, wuk, wuvt, tables):
    bsz, seq, d = x.shape
    cos, sin = tables
    out_shape, out_specs = _pre_out(bsz, seq)
    return pl.pallas_call(
        _mla_pre_kernel,
        out_shape=out_shape,
        grid=(bsz, seq // TS),
        in_specs=[
            pl.BlockSpec((1, TS, d), lambda b, s: (b, s, 0)),
            pl.BlockSpec((1, 1, N_MOD, d), lambda b, s: (layer, b, 0, 0)),
            _const_spec(g.shape),
            _const_spec(wdq.shape), _const_spec(gq.shape), _const_spec(wuq.shape),
            _const_spec(wuqr.shape),
            _const_spec(wdkv.shape), _const_spec(wdkvr.shape), _const_spec(gkv.shape),
            _const_spec(wuk.shape), _const_spec(wuvt.shape),
            pl.BlockSpec((1, TS, LANES), lambda b, s: (b, s, 0)),
            pl.BlockSpec((1, TS, LANES), lambda b, s: (b, s, 0)),
        ],
        out_specs=out_specs,
        compiler_params=_params(2),
        name="mla_pre",
    )(x, mod, g, wdq, gq, wuq, wuqr, wdkv, wdkvr, gkv, wuk, wuvt, cos, sin)


def _attn_kernel(q_ref, k_ref, vt_ref, o_ref):
    seq = q_ref.shape[1]
    key = lax.broadcasted_iota(jnp.int32, (MXU_TILE, MXU_TILE), 0)
    qry = lax.broadcasted_iota(jnp.int32, (MXU_TILE, MXU_TILE), 1)
    tri = key <= qry
    n_sub = TQ // MXU_TILE
    order = list(range(seq // TQ))
    blocks = [(h, qi) for h in range(ATTN_HEADS) for qi in (order if h % 2 == 0 else order[::-1])]

    def logits(h, qi, c, n):
        q = q_ref[0, TQ * qi:TQ * (qi + 1), LANES * h:LANES * (h + 1)]
        s = _dot_nt(k_ref[0, TQ * c:TQ * (c + n), LANES * h:LANES * (h + 1)], q)
        return [s[TQ * j:TQ * (j + 1)] for j in range(n)]

    def softmax_weights(s_chunks):
        diag = s_chunks[-1]
        tile = lambda i, j: diag[MXU_TILE * i:MXU_TILE * (i + 1), MXU_TILE * j:MXU_TILE * (j + 1)]
        d_tiles = [[tile(i, j) if i < j else jnp.where(tri, tile(i, j), -jnp.inf)
                    for i in range(j + 1)] for j in range(n_sub)]
        m = jnp.concatenate(
            [functools.reduce(jnp.maximum, [jnp.max(t, axis=0, keepdims=True) for t in col])
             for col in d_tiles], axis=1)
        for sc in s_chunks[:-1]:
            m = jnp.maximum(m, jnp.max(sc, axis=0, keepdims=True))
        p = [jnp.exp2((sc - m).astype(BF16)) for sc in s_chunks[:-1]]
        cols = []
        for j, col in enumerate(d_tiles):
            mj = m[:, MXU_TILE * j:MXU_TILE * (j + 1)]
            pt = [jnp.exp2((t - mj).astype(BF16)) for t in col]
            if j + 1 < n_sub:
                pt.append(jnp.zeros((MXU_TILE * (n_sub - 1 - j), MXU_TILE), BF16))
            cols.append(jnp.concatenate(pt, axis=0))
        return p + [jnp.concatenate(cols, axis=1)]

    def weighted_values(h, c, p):
        return _dot(vt_ref[0, V_ROWS * h:V_ROWS * (h + 1), TQ * c:TQ * (c + 1)], p)

    s_chunks = logits(*blocks[0], 0, blocks[0][1] + 1)
    for i, (h, qi) in enumerate(blocks):
        p_chunks = softmax_weights(s_chunks)
        nxt = blocks[i + 1] if i + 1 < len(blocks) else None
        n_next = nxt[1] + 1 if nxt else 0
        s_chunks, acc = [], 0.0
        for c in range(0, max(qi + 1, n_next), LOGITS_CHUNKS):
            if c < n_next:
                s_chunks += logits(*nxt, c, min(LOGITS_CHUNKS, n_next - c))
            for cc in range(c, min(c + LOGITS_CHUNKS, qi + 1)):
                acc = acc + weighted_values(h, cc, p_chunks[cc])
        o_ref[0, HEAD_DIM * h:HEAD_DIM * (h + 1), TQ * qi:TQ * (qi + 1)] = (
            acc[:HEAD_DIM] / acc[HEAD_DIM:HEAD_DIM + 1]).astype(BF16)


def _attn_call(qp, kp, vt):
    bsz, seq, _ = qp.shape
    return pl.pallas_call(
        _attn_kernel,
        out_shape=jax.ShapeDtypeStruct((bsz, D_MODEL, seq), BF16),
        grid=(bsz, HEADS // ATTN_HEADS),
        in_specs=[
            pl.BlockSpec((1, seq, ATTN_HEADS * LANES), lambda b, h: (b, 0, h)),
            pl.BlockSpec((1, seq, ATTN_HEADS * LANES), lambda b, h: (b, 0, h)),
            pl.BlockSpec((1, ATTN_HEADS * V_ROWS, seq), lambda b, h: (b, h, 0)),
        ],
        out_specs=pl.BlockSpec((1, ATTN_HEADS * HEAD_DIM, seq), lambda b, h: (b, h, 0)),
        compiler_params=_params(2),
        name="causal_attn",
    )(qp, kp, vt)


FF_CHUNK = 1024
POST_SPLIT = 2


def _post_kernel(ot_ref, x_ref, mod_ref, g_ref, wo_ref, w1_ref, w2_ref, gf_ref, out_ref,
                 *, final):
    g_m = mod_ref[0, 0, 2:3, :]
    sh_f = mod_ref[0, 0, 3:4, :]
    sc_f = mod_ref[0, 0, 4:5, :]
    g_f = mod_ref[0, 0, 5:6, :]
    ts = x_ref.shape[1]
    half = ts // POST_SPLIT
    x1 = [x_ref[0, half * i:half * (i + 1), :]
          + g_m * _dot_tn(ot_ref[0, :, half * i:half * (i + 1)], wo_ref[...])
          for i in range(POST_SPLIT)]
    for i in range(POST_SPLIT):
        hb = (_rms(x1[i], g_ref[...]) * (1.0 + sc_f) + sh_f).astype(BF16)
        acc = jnp.zeros_like(x1[i])
        for j in range(D_FF // FF_CHUNK):
            a = jnp.maximum(_dot(hb, w1_ref[:, FF_CHUNK * j:FF_CHUNK * (j + 1)]), 0.0)
            acc = acc + _dot((a * a).astype(BF16), w2_ref[FF_CHUNK * j:FF_CHUNK * (j + 1), :])
        x2 = x1[i] + g_f * acc
        if final:
            x2 = _rms(x2, gf_ref[...])
        out_ref[0, half * i:half * (i + 1), :] = x2


def _post_call(ot, x, mod, layer, g, wo, w1, w2, gf, final):
    bsz, seq, d = x.shape
    return pl.pallas_call(
        functools.partial(_post_kernel, final=final),
        out_shape=jax.ShapeDtypeStruct((bsz, seq, d), F32),
        grid=(bsz, seq // TS),
        in_specs=[
            pl.BlockSpec((1, d, TS), lambda b, s: (b, 0, s)),
            pl.BlockSpec((1, TS, d), lambda b, s: (b, s, 0)),
            pl.BlockSpec((1, 1, N_MOD, d), lambda b, s: (layer, b, 0, 0)),
            _const_spec(g.shape),
            _const_spec(wo.shape), _const_spec(w1.shape), _const_spec(w2.shape),
            _const_spec(gf.shape),
        ],
        out_specs=pl.BlockSpec((1, TS, d), lambda b, s: (b, s, 0)),
        compiler_params=_params(2),
        name="post_mlp",
    )(ot, x, mod, g, wo, w1, w2, gf)


def _pad_heads(w, used):
    k = w.shape[0]
    return jnp.pad(w.reshape(k, HEADS, used),
                   ((0, 0), (0, 0), (0, LANES - used))).reshape(k, HP)


def kernel(x, c, positions, ada_w, ada_b, norm_mix_g, norm_mlp_g, fox_w_in, fox_b_f, fox_w_out, mla_w_dq, mla_q_norm_g, mla_w_uq, mla_w_dkv, mla_kv_norm_g, mla_w_ukv, mla_w_out, mlp_w1, mlp_w2, final_norm_g):
    bsz, seq, d = x.shape
    assert d == D_MODEL and seq % TS == 0 and seq % TQ == 0
    assert ada_w.shape == (DEPTH, d, N_MOD * d)
    assert fox_w_in.shape[1:] == (d, 3 * d + HEADS)
    assert mla_w_uq.shape[1:] == (Q_RANK, HEADS * (HEAD_DIM + ROPE_DIM))
    assert mla_w_dkv.shape[1:] == (d, KV_RANK + ROPE_DIM)
    assert mla_w_ukv.shape[1:] == (KV_RANK, HEADS * 2 * HEAD_DIM)

    mod = _mod_call(c, ada_w, ada_b).reshape(DEPTH, bsz, N_MOD, d)
    tables = _rope_call(positions)
    gf = final_norm_g.reshape(1, d)

    for i in range(DEPTH):
        j = i // 2
        g_mix = norm_mix_g[i].reshape(1, d)
        if i % 2 == 0:
            w_in = fox_w_in[j]
            wqf = jnp.pad(jnp.concatenate([w_in[:, :d], w_in[:, 3 * d:]], axis=1),
                          ((0, 0), (0, LANES - HEADS))).astype(BF16)
            wk = w_in[:, d:2 * d].astype(BF16)
            wvt = w_in[:, 2 * d:3 * d].T.astype(BF16)
            bf = fox_b_f[j].reshape(1, HEADS)
            qp, kp, vt = _fox_pre_call(x, mod, i, g_mix, wqf, wk, wvt, bf)
            wo = fox_w_out[j].astype(BF16)
        else:
            wdq = mla_w_dq[j].astype(BF16)
            gq = mla_q_norm_g[j].reshape(1, Q_RANK)
            w_uq = mla_w_uq[j].reshape(Q_RANK, HEADS, HEAD_DIM + ROPE_DIM)
            wuq = _pad_heads(mla_w_uq[j], HEAD_DIM + ROPE_DIM).astype(BF16)
            wuqr = jnp.pad(
                jnp.concatenate([-w_uq[:, :, HEAD_DIM + ROPE_HALF:],
                                 w_uq[:, :, HEAD_DIM:HEAD_DIM + ROPE_HALF]], axis=2),
                ((0, 0), (0, 0), (HEAD_DIM, LANES - HEAD_DIM - ROPE_DIM)),
            ).reshape(Q_RANK, HP).astype(BF16)
            wdkv = mla_w_dkv[j][:, :KV_RANK].astype(BF16)
            wdkvr = jnp.pad(mla_w_dkv[j][:, KV_RANK:],
                            ((0, 0), (HEAD_DIM, LANES - HEAD_DIM - ROPE_DIM))).astype(BF16)
            gkv = mla_kv_norm_g[j].reshape(1, KV_RANK)
            w_ukv = mla_w_ukv[j].reshape(KV_RANK, HEADS, 2 * HEAD_DIM)
            wuk = _pad_heads(w_ukv[:, :, :HEAD_DIM].reshape(KV_RANK, HEADS * HEAD_DIM),
                             HEAD_DIM).astype(BF16)
            wuvt = w_ukv[:, :, HEAD_DIM:].reshape(KV_RANK, HEADS * HEAD_DIM).T.astype(BF16)
            qp, kp, vt = _mla_pre_call(x, mod, i, g_mix, wdq, gq, wuq, wuqr, wdkv, wdkvr, gkv,
                                       wuk, wuvt, tables)
            wo = mla_w_out[j].astype(BF16)
        ot = _attn_call(qp, kp, vt)
        x = _post_call(ot, x, mod, i, norm_mlp_g[i].reshape(1, d), wo,
                       mlp_w1[i].astype(BF16), mlp_w2[i].astype(BF16), gf,
                       final=(i == DEPTH - 1))
    return x
```

```python
import functools

import jax
import jax.numpy as jnp
from jax import lax
from jax.experimental import pallas as pl
from jax.experimental.pallas import tpu as pltpu

D_MODEL = 1024
DEPTH = 4
HEADS = 16
HEAD_DIM = 64
ROPE_DIM = 32
ROPE_HALF = ROPE_DIM // 2
Q_RANK = 256
KV_RANK = 128
D_FF = 4 * D_MODEL
ROPE_THETA = 10000.0
NORM_EPS = 1e-6
N_MOD = 6

LANES = 128
MXU_TILE = 256
HP = HEADS * LANES
N_SPLIT = 3

V_ROWS = HEAD_DIM + 16
LOG2E = 1.4426950408889634

TS = 512
TQ = 512
ATTN_HEADS = 4
HEAD_GROUPS = HEADS // ATTN_HEADS
GROUP_W = ATTN_HEADS * LANES
LOGITS_CHUNKS = 2
VMEM_LIMIT = 56 * 1024 * 1024

BF16 = jnp.bfloat16
F32 = jnp.float32
NT_DIMS = (((1,), (1,)), ((), ()))
TN_DIMS = (((0,), (0,)), ((), ()))


def _dot(a, b):
    return jnp.dot(a, b, preferred_element_type=F32)


def _dot_nt(a, b):
    return lax.dot_general(a, b, NT_DIMS, preferred_element_type=F32)


def _dot_tn(a, b):
    return lax.dot_general(a, b, TN_DIMS, preferred_element_type=F32)


def _dot_exact(a, b):
    return jnp.dot(a, b, preferred_element_type=F32, precision=lax.Precision.HIGHEST)


def _rms(x, g):
    ms = jnp.mean(x * x, axis=-1, keepdims=True)
    return x * lax.rsqrt(ms + NORM_EPS) * g


def _const_spec(shape):
    zeros = (0,) * len(shape)
    return pl.BlockSpec(shape, lambda *_: zeros, pipeline_mode=pl.Buffered(1))


def _params(n_axes):
    return pltpu.CompilerParams(
        dimension_semantics=("arbitrary",) * n_axes,
        vmem_limit_bytes=VMEM_LIMIT)


MOD_TN = 1024


def _mod_kernel(c_ref, w_ref, b_ref, o_ref):
    ca = jax.nn.silu(c_ref[...])
    o_ref[0] = _dot_exact(ca, w_ref[0]) + b_ref[0]


def _mod_call(c, ada_w, ada_b):
    depth, d, n = ada_w.shape
    bsz = c.shape[0]
    return pl.pallas_call(
        _mod_kernel,
        out_shape=jax.ShapeDtypeStruct((depth, bsz, n), F32),
        grid=(depth, n // MOD_TN),
        in_specs=[
            pl.BlockSpec((bsz, d), lambda l, j: (0, 0)),
            pl.BlockSpec((1, d, MOD_TN), lambda l, j: (l, 0, j)),
            pl.BlockSpec((1, 1, MOD_TN), lambda l, j: (l, 0, j)),
        ],
        out_specs=pl.BlockSpec((1, bsz, MOD_TN), lambda l, j: (l, 0, j)),
        compiler_params=_params(2),
        name="adaln_mod",
    )(c, ada_w, ada_b.reshape(depth, 1, n))


def _rope_kernel(pos_ref, c_ref, s_ref):
    seq = pos_ref.shape[1]
    lane = lax.broadcasted_iota(jnp.int32, (1, LANES), 1)
    is_rope = (lane >= HEAD_DIM) & (lane < HEAD_DIM + ROPE_DIM)
    fidx = ((lane - HEAD_DIM) % ROPE_HALF).astype(F32)
    inv_freq = ROPE_THETA ** (-(2.0 * fidx) / ROPE_DIM)
    ang = pos_ref[0].astype(F32) * inv_freq
    rope_mask = jnp.broadcast_to(is_rope, (seq, LANES))
    pass_mask = jnp.broadcast_to(lane < HEAD_DIM, (seq, LANES))
    c_ref[0] = jnp.where(rope_mask, jnp.cos(ang), jnp.where(pass_mask, 1.0, 0.0))
    s_ref[0] = jnp.where(rope_mask, jnp.sin(ang), 0.0)


def _rope_call(positions):
    bsz, seq = positions.shape
    return pl.pallas_call(
        _rope_kernel,
        out_shape=(
            jax.ShapeDtypeStruct((bsz, seq, LANES), F32),
            jax.ShapeDtypeStruct((bsz, seq, LANES), F32),
        ),
        grid=(bsz,),
        in_specs=[pl.BlockSpec((1, seq, 1), lambda b: (b, 0, 0))],
        out_specs=(
            pl.BlockSpec((1, seq, LANES), lambda b: (b, 0, 0)),
            pl.BlockSpec((1, seq, LANES), lambda b: (b, 0, 0)),
        ),
        compiler_params=_params(1),
        name="rope_tables",
    )(positions.reshape(bsz, seq, 1))


def _modulated_norm(x_ref, mod_ref, g_ref, shift_row):
    x = x_ref[0]
    sh = mod_ref[0, 0, shift_row:shift_row + 1, :]
    sc = mod_ref[0, 0, shift_row + 1:shift_row + 2, :]
    return _rms(x, g_ref[...]) * (1.0 + sc) + sh


def _split3(x):
    hi = x.astype(BF16)
    r1 = x - hi.astype(F32)
    mid = r1.astype(BF16)
    lo = (r1 - mid.astype(F32)).astype(BF16)
    return hi, mid, lo


def _log_sigmoid(x):
    return jnp.minimum(x, 0.0) - jnp.log1p(jnp.exp(-jnp.abs(x)))


def _rope_rotate(blk, cos, sin, first_half):
    rot = jnp.where(first_half,
                    -pltpu.roll(blk, LANES - ROPE_HALF, axis=1),
                    pltpu.roll(blk, ROPE_HALF, axis=1))
    return blk * cos + rot * sin


def _head_tile(hd):
    lane0 = LANES * (hd % ATTN_HEADS)
    return (0, hd // ATTN_HEADS, slice(None), slice(lane0, lane0 + LANES))


def _store_vt(vt_ref, v_t):
    ts = v_t.shape[1]
    sub = lax.broadcasted_iota(jnp.int32, (V_ROWS - HEAD_DIM, ts), 0)
    ones_rows = jnp.where(sub == 0, 1.0, 0.0).astype(BF16)
    for hd in range(HEADS):
        r0 = V_ROWS * hd
        vt_ref[0, r0:r0 + HEAD_DIM, :] = v_t[HEAD_DIM * hd:HEAD_DIM * (hd + 1), :].astype(BF16)
        vt_ref[0, r0 + HEAD_DIM:r0 + V_ROWS, :] = ones_rows


def _fox_pre_kernel(x_ref, mod_ref, g_ref, wqf_ref, wk_ref, wvt_ref, bf_ref,
                    qp_ref, kp_ref, vt_ref, carry_ref):
    ts = x_ref.shape[1]

    @pl.when(pl.program_id(1) == 0)
    def _():
        carry_ref[...] = jnp.zeros_like(carry_ref)

    hb = _modulated_norm(x_ref, mod_ref, g_ref, 0).astype(BF16)
    d = hb.shape[1]
    qf = _dot(hb, wqf_ref[...])
    k = _dot(hb, wk_ref[...])
    q = qf[:, :d] * (HEAD_DIM ** -0.5 * LOG2E)

    lf = _log_sigmoid(qf[:, d:d + HEADS] + bf_ref[...])
    r = lax.broadcasted_iota(jnp.int32, (ts, ts), 0)
    c = lax.broadcasted_iota(jnp.int32, (ts, ts), 1)
    lower = jnp.where(c <= r, 1.0, 0.0).astype(BF16)
    sums = _dot(lower, jnp.concatenate(_split3(lf), axis=1))
    f_cum = carry_ref[...]
    for j in range(N_SPLIT):
        f_cum = f_cum + sums[:, HEADS * j:HEADS * (j + 1)]
    carry_ref[...] = f_cum[ts - 1:ts, :]

    row = lax.broadcasted_iota(jnp.int32, (N_SPLIT * HEADS, LANES), 0)
    lsel = (lax.broadcasted_iota(jnp.int32, (N_SPLIT * HEADS, LANES), 1)
            - 2 * N_SPLIT * (row % HEADS) - row // HEADS)
    sel = jnp.where(lsel == 0, 1.0, jnp.where(lsel == N_SPLIT, -1.0, 0.0)).astype(BF16)
    g = _dot(jnp.concatenate(_split3(f_cum * LOG2E), axis=1), sel)
    v_t = _dot_nt(wvt_ref[...], hb)

    lane = lax.broadcasted_iota(jnp.int32, (ts, LANES), 1)
    in_head = lane < HEAD_DIM
    in_pos = lane < HEAD_DIM + N_SPLIT
    in_ext = lane < HEAD_DIM + 2 * N_SPLIT

    def store_heads(out_ref, proj, is_q):
        for hd in range(HEADS):
            blk = proj[:, LANES * (hd // 2):LANES * (hd // 2 + 1)]
            if hd % 2 == 1:
                blk = pltpu.roll(blk, HEAD_DIM, axis=1)
            rot = pltpu.roll(g, (HEAD_DIM - 2 * N_SPLIT * hd) % LANES, axis=1)
            if is_q:
                ext = jnp.where(in_pos, rot, jnp.where(in_ext, 1.0, 0.0))
            else:
                ext = jnp.where(in_pos, 1.0, jnp.where(in_ext, rot, 0.0))
            out_ref[_head_tile(hd)] = jnp.where(in_head, blk, ext).astype(BF16)

    store_heads(qp_ref, q, True)
    store_heads(kp_ref, k, False)
    _store_vt(vt_ref, v_t)


def _pre_out(bsz, seq):
    out_shape = (
        jax.ShapeDtypeStruct((bsz, HEAD_GROUPS, seq, GROUP_W), BF16),
        jax.ShapeDtypeStruct((bsz, HEAD_GROUPS, seq, GROUP_W), BF16),
        jax.ShapeDtypeStruct((bsz, HEADS * V_ROWS, seq), BF16),
    )
    out_specs = (
        pl.BlockSpec((1, HEAD_GROUPS, TS, GROUP_W), lambda b, s: (b, 0, s, 0)),
        pl.BlockSpec((1, HEAD_GROUPS, TS, GROUP_W), lambda b, s: (b, 0, s, 0)),
        pl.BlockSpec((1, HEADS * V_ROWS, TS), lambda b, s: (b, 0, s)),
    )
    return out_shape, out_specs


def _fox_pre_call(x, mod, layer, g, wqf, wk, wvt, bf):
    bsz, seq, d = x.shape
    out_shape, out_specs = _pre_out(bsz, seq)
    return pl.pallas_call(
        _fox_pre_kernel,
        out_shape=out_shape,
        grid=(bsz, seq // TS),
        in_specs=[
            pl.BlockSpec((1, TS, d), lambda b, s: (b, s, 0)),
            pl.BlockSpec((1, 1, N_MOD, d), lambda b, s: (layer, b, 0, 0)),
            _const_spec(g.shape),
            _const_spec(wqf.shape), _const_spec(wk.shape), _const_spec(wvt.shape),
            _const_spec(bf.shape),
        ],
        out_specs=out_specs,
        scratch_shapes=[pltpu.VMEM((1, HEADS), F32)],
        compiler_params=_params(2),
        name="fox_pre",
    )(x, mod, g, wqf, wk, wvt, bf)


def _mla_pre_kernel(x_ref, mod_ref, g_ref, wdq_ref, gq_ref, wuq_ref, wuqr_ref, wdkv_ref, wdkvr_ref,
                    gkv_ref, wuk_ref, wuvt_ref, cos_ref, sin_ref, qp_ref, kp_ref, vt_ref):
    ts = x_ref.shape[1]
    hb = _modulated_norm(x_ref, mod_ref, g_ref, 0).astype(BF16)
    cos = cos_ref[0]
    sin = sin_ref[0]
    lane = lax.broadcasted_iota(jnp.int32, (ts, LANES), 1)
    first_half = lane < HEAD_DIM + ROPE_HALF
    scale = (HEAD_DIM + ROPE_DIM) ** -0.5 * LOG2E

    cq = _rms(_dot(hb, wdq_ref[...]), gq_ref[...]).astype(BF16)
    ckv = _rms(_dot(hb, wdkv_ref[...]), gkv_ref[...]).astype(BF16)
    k_rope = _rope_rotate(_dot(hb, wdkvr_ref[...]), cos, sin, first_half)
    q = _dot(cq, wuq_ref[...])
    q_rot = _dot(cq, wuqr_ref[...])
    k_nope = _dot(ckv, wuk_ref[...])
    for hd in range(HEADS):
        out = slice(LANES * hd, LANES * (hd + 1))
        qp_ref[_head_tile(hd)] = ((q[:, out] * cos + q_rot[:, out] * sin) * scale).astype(BF16)
    v_t = _dot_nt(wuvt_ref[...], ckv)
    for hd in range(HEADS):
        out = slice(LANES * hd, LANES * (hd + 1))
        kp_ref[_head_tile(hd)] = (k_nope[:, out] + k_rope).astype(BF16)
    _store_vt(vt_ref, v_t)


def _mla_pre_call(x, mod, layer, g, wdq, gq, wuq, wuqr, wdkv, wdkvr, gkv, wuk, wuvt, tables):
    bsz, seq, d = x.shape
    cos, sin = tables
    out_shape, out_specs = _pre_out(bsz, seq)
    return pl.pallas_call(
        _mla_pre_kernel,
        out_shape=out_shape,
        grid=(bsz, seq // TS),
        in_specs=[
            pl.BlockSpec((1, TS, d), lambda b, s: (b, s, 0)),
            pl.BlockSpec((1, 1, N_MOD, d), lambda b, s: (layer, b, 0, 0)),
            _const_spec(g.shape),
            _const_spec(wdq.shape), _const_spec(gq.shape), _const_spec(wuq.shape),
            _const_spec(wuqr.shape),
            _const_spec(wdkv.shape), _const_spec(wdkvr.shape), _const_spec(gkv.shape),
            _const_spec(wuk.shape), _const_spec(wuvt.shape),
            pl.BlockSpec((1, TS, LANES), lambda b, s: (b, s, 0)),
            pl.BlockSpec((1, TS, LANES), lambda b, s: (b, s, 0)),
        ],
        out_specs=out_specs,
        compiler_params=_params(2),
        name="mla_pre",
    )(x, mod, g, wdq, gq, wuq, wuqr, wdkv, wdkvr, gkv, wuk, wuvt, cos, sin)


def _attn_kernel(q_ref, k_ref, vt_ref, o_ref):
    seq = q_ref.shape[2]
    key = lax.broadcasted_iota(jnp.int32, (MXU_TILE, MXU_TILE), 0)
    qry = lax.broadcasted_iota(jnp.int32, (MXU_TILE, MXU_TILE), 1)
    tri = key <= qry
    n_sub = TQ // MXU_TILE
    order = list(range(seq // TQ))
    blocks = [(h, qi) for h in range(ATTN_HEADS) for qi in (order if h % 2 == 0 else order[::-1])]

    def logits(h, qi, c, n):
        q = q_ref[0, 0, TQ * qi:TQ * (qi + 1), LANES * h:LANES * (h + 1)]
        s = _dot_nt(k_ref[0, 0, TQ * c:TQ * (c + n), LANES * h:LANES * (h + 1)], q)
        return [s[TQ * j:TQ * (j + 1)] for j in range(n)]

    def softmax_weights(s_chunks):
        diag = s_chunks[-1]
        tile = lambda i, j: diag[MXU_TILE * i:MXU_TILE * (i + 1), MXU_TILE * j:MXU_TILE * (j + 1)]
        d_tiles = [[tile(i, j) if i < j else jnp.where(tri, tile(i, j), -jnp.inf)
                    for i in range(j + 1)] for j in range(n_sub)]
        m = jnp.concatenate(
            [functools.reduce(jnp.maximum, [jnp.max(t, axis=0, keepdims=True) for t in col])
             for col in d_tiles], axis=1)
        for sc in s_chunks[:-1]:
            m = jnp.maximum(m, jnp.max(sc, axis=0, keepdims=True))
        p = [jnp.exp2((sc - m).astype(BF16)) for sc in s_chunks[:-1]]
        cols = []
        for j, col in enumerate(d_tiles):
            mj = m[:, MXU_TILE * j:MXU_TILE * (j + 1)]
            pt = [jnp.exp2((t - mj).astype(BF16)) for t in col]
            if j + 1 < n_sub:
                pt.append(jnp.zeros((MXU_TILE * (n_sub - 1 - j), MXU_TILE), BF16))
            cols.append(jnp.concatenate(pt, axis=0))
        return p + [jnp.concatenate(cols, axis=1)]

    def weighted_values(h, c, p):
        return _dot(vt_ref[0, V_ROWS * h:V_ROWS * (h + 1), TQ * c:TQ * (c + 1)], p)

    s_chunks = logits(*blocks[0], 0, blocks[0][1] + 1)
    for i, (h, qi) in enumerate(blocks):
        p_chunks = softmax_weights(s_chunks)
        nxt = blocks[i + 1] if i + 1 < len(blocks) else None
        n_next = nxt[1] + 1 if nxt else 0
        s_chunks, acc = [], 0.0
        for c in range(0, max(qi + 1, n_next), LOGITS_CHUNKS):
            if c < n_next:
                s_chunks += logits(*nxt, c, min(LOGITS_CHUNKS, n_next - c))
            for cc in range(c, min(c + LOGITS_CHUNKS, qi + 1)):
                acc = acc + weighted_values(h, cc, p_chunks[cc])
        o_ref[0, HEAD_DIM * h:HEAD_DIM * (h + 1), TQ * qi:TQ * (qi + 1)] = (
            acc[:HEAD_DIM] / acc[HEAD_DIM:HEAD_DIM + 1]).astype(BF16)


def _attn_call(qp, kp, vt):
    bsz, _, seq, _ = qp.shape
    return pl.pallas_call(
        _attn_kernel,
        out_shape=jax.ShapeDtypeStruct((bsz, D_MODEL, seq), BF16),
        grid=(bsz, HEAD_GROUPS),
        in_specs=[
            pl.BlockSpec((1, 1, seq, GROUP_W), lambda b, h: (b, h, 0, 0)),
            pl.BlockSpec((1, 1, seq, GROUP_W), lambda b, h: (b, h, 0, 0)),
            pl.BlockSpec((1, ATTN_HEADS * V_ROWS, seq), lambda b, h: (b, h, 0)),
        ],
        out_specs=pl.BlockSpec((1, ATTN_HEADS * HEAD_DIM, seq), lambda b, h: (b, h, 0)),
        compiler_params=_params(2),
        name="causal_attn",
    )(qp, kp, vt)


FF_CHUNK = 1024
POST_SPLIT = 2


def _post_kernel(ot_ref, x_ref, mod_ref, g_ref, wo_ref, w1_ref, w2_ref, gf_ref, out_ref,
                 *, final):
    g_m = mod_ref[0, 0, 2:3, :]
    sh_f = mod_ref[0, 0, 3:4, :]
    sc_f = mod_ref[0, 0, 4:5, :]
    g_f = mod_ref[0, 0, 5:6, :]
    ts = x_ref.shape[1]
    half = ts // POST_SPLIT
    x1 = [x_ref[0, half * i:half * (i + 1), :]
          + g_m * _dot_tn(ot_ref[0, :, half * i:half * (i + 1)], wo_ref[...])
          for i in range(POST_SPLIT)]
    for i in range(POST_SPLIT):
        hb = (_rms(x1[i], g_ref[...]) * (1.0 + sc_f) + sh_f).astype(BF16)
        acc = jnp.zeros_like(x1[i])
        for j in range(D_FF // FF_CHUNK):
            a = jnp.maximum(_dot(hb, w1_ref[:, FF_CHUNK * j:FF_CHUNK * (j + 1)]), 0.0)
            acc = acc + _dot((a * a).astype(BF16), w2_ref[FF_CHUNK * j:FF_CHUNK * (j + 1), :])
        x2 = x1[i] + g_f * acc
        if final:
            x2 = _rms(x2, gf_ref[...])
        out_ref[0, half * i:half * (i + 1), :] = x2


def _post_call(ot, x, mod, layer, g, wo, w1, w2, gf, final):
    bsz, seq, d = x.shape
    return pl.pallas_call(
        functools.partial(_post_kernel, final=final),
        out_shape=jax.ShapeDtypeStruct((bsz, seq, d), F32),
        grid=(bsz, seq // TS),
        in_specs=[
            pl.BlockSpec((1, d, TS), lambda b, s: (b, 0, s)),
            pl.BlockSpec((1, TS, d), lambda b, s: (b, s, 0)),
            pl.BlockSpec((1, 1, N_MOD, d), lambda b, s: (layer, b, 0, 0)),
            _const_spec(g.shape),
            _const_spec(wo.shape), _const_spec(w1.shape), _const_spec(w2.shape),
            _const_spec(gf.shape),
        ],
        out_specs=pl.BlockSpec((1, TS, d), lambda b, s: (b, s, 0)),
        compiler_params=_params(2),
        name="post_mlp",
    )(ot, x, mod, g, wo, w1, w2, gf)


def _pad_heads(w, used):
    k = w.shape[0]
    return jnp.pad(w.reshape(k, HEADS, used),
                   ((0, 0), (0, 0), (0, LANES - used))).reshape(k, HP)


def kernel(x, c, positions, ada_w, ada_b, norm_mix_g, norm_mlp_g, fox_w_in, fox_b_f, fox_w_out, mla_w_dq, mla_q_norm_g, mla_w_uq, mla_w_dkv, mla_kv_norm_g, mla_w_ukv, mla_w_out, mlp_w1, mlp_w2, final_norm_g):
    bsz, seq, d = x.shape
    assert d == D_MODEL and seq % TS == 0 and seq % TQ == 0
    assert ada_w.shape == (DEPTH, d, N_MOD * d)
    assert fox_w_in.shape[1:] == (d, 3 * d + HEADS)
    assert mla_w_uq.shape[1:] == (Q_RANK, HEADS * (HEAD_DIM + ROPE_DIM))
    assert mla_w_dkv.shape[1:] == (d, KV_RANK + ROPE_DIM)
    assert mla_w_ukv.shape[1:] == (KV_RANK, HEADS * 2 * HEAD_DIM)

    mod = _mod_call(c, ada_w, ada_b).reshape(DEPTH, bsz, N_MOD, d)
    tables = _rope_call(positions)
    gf = final_norm_g.reshape(1, d)

    for i in range(DEPTH):
        j = i // 2
        g_mix = norm_mix_g[i].reshape(1, d)
        if i % 2 == 0:
            w_in = fox_w_in[j]
            wqf = jnp.pad(jnp.concatenate([w_in[:, :d], w_in[:, 3 * d:]], axis=1),
                          ((0, 0), (0, LANES - HEADS))).astype(BF16)
            wk = w_in[:, d:2 * d].astype(BF16)
            wvt = w_in[:, 2 * d:3 * d].T.astype(BF16)
            bf = fox_b_f[j].reshape(1, HEADS)
            qp, kp, vt = _fox_pre_call(x, mod, i, g_mix, wqf, wk, wvt, bf)
            wo = fox_w_out[j].astype(BF16)
        else:
            wdq = mla_w_dq[j].astype(BF16)
            gq = mla_q_norm_g[j].reshape(1, Q_RANK)
            w_uq = mla_w_uq[j].reshape(Q_RANK, HEADS, HEAD_DIM + ROPE_DIM)
            wuq = _pad_heads(mla_w_uq[j], HEAD_DIM + ROPE_DIM).astype(BF16)
            wuqr = jnp.pad(
                jnp.concatenate([-w_uq[:, :, HEAD_DIM + ROPE_HALF:],
                                 w_uq[:, :, HEAD_DIM:HEAD_DIM + ROPE_HALF]], axis=2),
                ((0, 0), (0, 0), (HEAD_DIM, LANES - HEAD_DIM - ROPE_DIM)),
            ).reshape(Q_RANK, HP).astype(BF16)
            wdkv = mla_w_dkv[j][:, :KV_RANK].astype(BF16)
            wdkvr = jnp.pad(mla_w_dkv[j][:, KV_RANK:],
                            ((0, 0), (HEAD_DIM, LANES - HEAD_DIM - ROPE_DIM))).astype(BF16)
            gkv = mla_kv_norm_g[j].reshape(1, KV_RANK)
            w_ukv = mla_w_ukv[j].reshape(KV_RANK, HEADS, 2 * HEAD_DIM)
            wuk = _pad_heads(w_ukv[:, :, :HEAD_DIM].reshape(KV_RANK, HEADS * HEAD_DIM),
                             HEAD_DIM).astype(BF16)
            wuvt = w_ukv[:, :, HEAD_DIM:].reshape(KV_RANK, HEADS * HEAD_DIM).T.astype(BF16)
            qp, kp, vt = _mla_pre_call(x, mod, i, g_mix, wdq, gq, wuq, wuqr, wdkv, wdkvr, gkv,
                                       wuk, wuvt, tables)
            wo = mla_w_out[j].astype(BF16)
        ot = _attn_call(qp, kp, vt)
        x = _post_call(ot, x, mod, i, norm_mlp_g[i].reshape(1, d), wo,
                       mlp_w1[i].astype(BF16), mlp_w2[i].astype(BF16), gf,
                       final=(i == DEPTH - 1))
    return x
```
